```python
import math
import jax, jax.numpy as jnp
from jax import lax
import numpy as np

D_MODEL = 1024
BATCH = 16
SEQ = 2048
DEPTH = 2

N_A = DEPTH // 2
N_B = DEPTH - N_A
CONV_W = 3
N_HEADS = 16
HEAD_DIM = D_MODEL // N_HEADS
ATT_DIM = N_HEADS * HEAD_DIM
MOBA_BLOCK = 256
MOBA_TOP_K = 3
Q_CHUNK = 128
N_BUCKETS = 32
MAX_EXACT = N_BUCKETS // 2
REL_MAX_DIST = 128
D_FF = 2816
FFN_CONV_W = 3
EPS = 1e-6
NEG = -1e30

kernel_name = "yoco_shortconv_moba_convglu"


def rms_norm(x, g):
    xf = x.astype(jnp.float32)
    y = xf * lax.rsqrt(jnp.mean(xf * xf, axis=-1, keepdims=True) + EPS)
    return (y * g.astype(jnp.float32)).astype(x.dtype)


def causal_dwconv(x, w):
    c = x.shape[-1]
    width = w.shape[0]
    return lax.conv_general_dilated(
        x, w[:, None, :].astype(x.dtype), window_strides=(1,),
        padding=[(width - 1, 0)], dimension_numbers=("NWC", "WIO", "NWC"),
        feature_group_count=c)


def short_conv_mixer(h, w_in, conv_w, w_out):
    b_gate, c_gate, hx = jnp.split(h @ w_in, 3, axis=-1)
    return (b_gate * causal_dwconv(c_gate * hx, conv_w)) @ w_out


def conv_glu_ffn(h, w_up, conv_w, conv_b, w_down):
    u = causal_dwconv(h @ w_up, conv_w) + conv_b
    g, v = jnp.split(u, 2, axis=-1)
    return (jax.nn.silu(g) * v) @ w_down


def rel_bucket(dist):
    n = jnp.maximum(dist, 0)
    is_small = n < MAX_EXACT
    nf = jnp.maximum(n, 1).astype(jnp.float32)
    large = MAX_EXACT + (jnp.log(nf / MAX_EXACT) / math.log(REL_MAX_DIST / MAX_EXACT)
                         * (N_BUCKETS - MAX_EXACT)).astype(jnp.int32)
    large = jnp.minimum(large, N_BUCKETS - 1)
    return jnp.where(is_small, n, large)


def shared_kv(x, kv_norm, w_k, w_v):
    h = rms_norm(x, kv_norm)
    bsz, s, _ = h.shape
    nbp = -(-s // MOBA_BLOCK)
    pad = nbp * MOBA_BLOCK - s
    k = (h @ w_k).reshape(bsz, s, N_HEADS, HEAD_DIM)
    v = (h @ w_v).reshape(bsz, s, N_HEADS, HEAD_DIM)
    k = jnp.pad(k, ((0, 0), (0, pad), (0, 0), (0, 0)))
    v = jnp.pad(v, ((0, 0), (0, pad), (0, 0), (0, 0)))
    k_blk = k.reshape(bsz, nbp, MOBA_BLOCK, N_HEADS, HEAD_DIM).transpose(0, 3, 1, 2, 4)
    v_blk = v.reshape(bsz, nbp, MOBA_BLOCK, N_HEADS, HEAD_DIM).transpose(0, 3, 1, 2, 4)
    count = jnp.clip(s - jnp.arange(nbp) * MOBA_BLOCK, 1, MOBA_BLOCK).astype(jnp.float32)
    k_mean = (k_blk.astype(jnp.float32).sum(axis=3) / count[:, None]).astype(k.dtype)
    return k_blk, v_blk, k_mean


def moba_attention(h, w_q, w_o, k_blk, v_blk, k_mean, rel_bias):
    bsz, s, _ = h.shape
    q = (h @ w_q).reshape(bsz, s, N_HEADS, HEAD_DIM).transpose(0, 2, 1, 3)
    nbp = k_blk.shape[2]
    k_sel = min(MOBA_TOP_K, nbp)
    n_chunks = s // Q_CHUNK
    scale = HEAD_DIM ** -0.5
    rb_t = rel_bias.T
    hi = jnp.arange(N_HEADS)
    own_slot = jnp.arange(k_sel + 1) == k_sel
    in_blk = jnp.arange(MOBA_BLOCK)

    def one_chunk(i):
        b = i // n_chunks
        q0 = (i % n_chunks) * Q_CHUNK
        qc = lax.dynamic_slice(q, (b, 0, q0, 0), (1, N_HEADS, Q_CHUNK, HEAD_DIM))[0]
        kb = lax.dynamic_index_in_dim(k_blk, b, 0, keepdims=False)
        vb = lax.dynamic_index_in_dim(v_blk, b, 0, keepdims=False)
        km = lax.dynamic_index_in_dim(k_mean, b, 0, keepdims=False)
        q_pos = q0 + jnp.arange(Q_CHUNK)
        own = q_pos // MOBA_BLOCK
        gate = jnp.einsum("hqd,hnd->hqn", qc, km).astype(jnp.float32)
        past = jnp.arange(nbp)[None, :] < own[:, None]
        gate = jnp.where(past[None], gate, NEG)
        _, top = lax.top_k(gate, k_sel)
        own_b = jnp.broadcast_to(own[None, :, None], (N_HEADS, Q_CHUNK, 1))
        idx = jnp.concatenate([top, own_b.astype(top.dtype)], axis=-1)
        kg = kb[hi[:, None, None], idx]
        vg = vb[hi[:, None, None], idx]
        sc = jnp.einsum("hqd,hqnkd->hqnk", qc, kg).astype(jnp.float32) * scale
        k_pos = idx[..., None] * MOBA_BLOCK + in_blk
        dist = q_pos[None, :, None, None] - k_pos
        sc = sc + rb_t[hi[:, None, None, None], rel_bucket(dist)].astype(jnp.float32)
        valid = jnp.where(own_slot[None, None, :, None], dist >= 0,
                          (idx < own[None, :, None])[..., None])
        sc = jnp.where(valid, sc, NEG)
        p = jax.nn.softmax(sc.reshape(N_HEADS, Q_CHUNK, -1), axis=-1).reshape(sc.shape)
        return jnp.einsum("hqnk,hqnkd->hqd", p.astype(vg.dtype), vg)

    o = lax.map(one_chunk, jnp.arange(bsz * n_chunks))
    o = o.reshape(bsz, n_chunks, N_HEADS, Q_CHUNK, HEAD_DIM).transpose(0, 1, 3, 2, 4)
    return o.reshape(bsz, s, ATT_DIM) @ w_o


def setup_inputs(seed: int = 0) -> dict:
    key = jax.random.key(seed)
    ks = jax.random.split(key, 20)
    f32 = jnp.float32
    nrm = lambda k, shape, sc: (jax.random.normal(k, shape, f32) * sc)
    gain = lambda k, shape: 1.0 + 0.05 * jax.random.normal(k, shape, f32)
    return {
        "x": jax.random.normal(ks[0], (BATCH, SEQ, D_MODEL), f32),
        "a_norm": gain(ks[1], (N_A, D_MODEL)),
        "a_w_in": nrm(ks[2], (N_A, D_MODEL, 3 * D_MODEL), D_MODEL ** -0.5),
        "a_conv": nrm(ks[3], (N_A, CONV_W, D_MODEL), CONV_W ** -0.5),
        "a_w_out": nrm(ks[4], (N_A, D_MODEL, D_MODEL), D_MODEL ** -0.5),
        "kv_norm": gain(ks[5], (D_MODEL,)),
        "w_k": nrm(ks[6], (D_MODEL, ATT_DIM), D_MODEL ** -0.5),
        "w_v": nrm(ks[7], (D_MODEL, ATT_DIM), D_MODEL ** -0.5),
        "b_norm": gain(ks[8], (N_B, D_MODEL)),
        "b_w_q": nrm(ks[9], (N_B, D_MODEL, ATT_DIM), D_MODEL ** -0.5),
        "b_w_o": nrm(ks[10], (N_B, ATT_DIM, D_MODEL), ATT_DIM ** -0.5),
        "rel_bias": nrm(ks[11], (N_BUCKETS, N_HEADS), 0.5),
        "f_norm": gain(ks[12], (DEPTH, D_MODEL)),
        "f_w_up": nrm(ks[13], (DEPTH, D_MODEL, 2 * D_FF), D_MODEL ** -0.5),
        "f_conv": nrm(ks[14], (DEPTH, FFN_CONV_W, 2 * D_FF), FFN_CONV_W ** -0.5),
        "f_conv_b": nrm(ks[15], (DEPTH, 2 * D_FF), 0.02),
        "f_w_down": nrm(ks[16], (DEPTH, D_FF, D_MODEL), D_FF ** -0.5),
        "final_norm": gain(ks[17], (D_MODEL,)),
    }


def reference(x, a_norm, a_w_in, a_conv, a_w_out, kv_norm, w_k, w_v, b_norm, b_w_q,
              b_w_o, rel_bias, f_norm, f_w_up, f_conv, f_conv_b, f_w_down, final_norm):
    k_blk = v_blk = k_mean = None
    for layer in range(DEPTH):
        if layer < N_A:
            i = layer
            x = x + short_conv_mixer(rms_norm(x, a_norm[i]), a_w_in[i], a_conv[i], a_w_out[i])
        else:
            if layer == N_A:
                k_blk, v_blk, k_mean = shared_kv(x, kv_norm, w_k, w_v)
            j = layer - N_A
            x = x + moba_attention(rms_norm(x, b_norm[j]), b_w_q[j], b_w_o[j],
                                   k_blk, v_blk, k_mean, rel_bias)
        x = x + conv_glu_ffn(rms_norm(x, f_norm[layer]), f_w_up[layer], f_conv[layer],
                             f_conv_b[layer], f_w_down[layer])
    return rms_norm(x, final_norm)
```

```python
import functools
import math

import numpy as np
import jax
import jax.numpy as jnp
from jax import lax
from jax.experimental import pallas as pl
from jax.experimental.pallas import tpu as pltpu

D_MODEL = 1024
N_HEADS = 16
HEAD_DIM = 64
MOBA_BLOCK = 256
MOBA_TOP_K = 3
N_BUCKETS = 32
MAX_EXACT = N_BUCKETS // 2
REL_MAX_DIST = 128
D_FF = 2816
EPS = 1e-6
NEG = -1e30

SUBLANES = 8
LANES = 128
HEADS_PER_STEP = LANES // HEAD_DIM
VMEM_LIMIT_BYTES = 56 * 1024 * 1024

TOKEN_TILE = 512
FF_CHUNK = 256

BF16 = jnp.bfloat16
F32 = jnp.float32


def _dot(a, b):
    return jnp.dot(a, b, preferred_element_type=F32)


def _dot_nt(a, b):
    return lax.dot_general(a, b, (((1,), (1,)), ((), ())), preferred_element_type=F32)


def _rms_scale(x):
    return lax.rsqrt(jnp.mean(x * x, axis=-1, keepdims=True) + EPS)


def _const_spec(shape):
    return pl.BlockSpec(shape, lambda *_: (0,) * len(shape), pipeline_mode=pl.Buffered(1))


def _params(n_axes):
    return pltpu.CompilerParams(
        dimension_semantics=("arbitrary",) * n_axes,
        vmem_limit_bytes=VMEM_LIMIT_BYTES)


def _load_carry(scr, tm):
    i = pl.program_id(1)

    @pl.when(i == 0)
    def _():
        scr[0:SUBLANES, :] = jnp.zeros((SUBLANES, scr.shape[1]), F32)

    @pl.when(i > 0)
    def _():
        scr[0:SUBLANES, :] = scr[tm:tm + SUBLANES, :]


def _causal_conv3(scr, w, tm):
    return (w[0:1, :] * scr[SUBLANES - 2:SUBLANES - 2 + tm, :]
            + w[1:2, :] * scr[SUBLANES - 1:SUBLANES - 1 + tm, :]
            + w[2:3, :] * scr[SUBLANES:SUBLANES + tm, :])


def _mixer_kernel(x_ref, g_ref, win_ref, cw_ref, wout_ref, o_ref, z_scr):
    tm, d = x_ref.shape
    _load_carry(z_scr, tm)
    x = x_ref[...]
    h = ((x * _rms_scale(x)) * g_ref[...]).astype(BF16)
    b_gate = _dot(h, win_ref[:, 0:d])
    c_gate = _dot(h, win_ref[:, d:2 * d])
    hx = _dot(h, win_ref[:, 2 * d:3 * d])
    z_scr[SUBLANES:SUBLANES + tm, :] = c_gate * hx
    y = (b_gate * _causal_conv3(z_scr, cw_ref[...], tm)).astype(BF16)
    o_ref[...] = x + _dot(y, wout_ref[...])


def _mixer(x, g, w_in, conv_w, w_out):
    bsz, s, d = x.shape
    tm = TOKEN_TILE
    tile = pl.BlockSpec((None, tm, d), lambda b, i: (b, i, 0))
    return pl.pallas_call(
        _mixer_kernel,
        grid=(bsz, s // tm),
        in_specs=[tile, _const_spec((1, d)), _const_spec((d, 3 * d)),
                  _const_spec((3, d)), _const_spec((d, d))],
        out_specs=tile,
        out_shape=jax.ShapeDtypeStruct(x.shape, F32),
        scratch_shapes=[pltpu.VMEM((tm + SUBLANES, d), F32)],
        compiler_params=_params(2),
        name="short_conv_mixer",
    )(x, g, w_in, conv_w, w_out)


def _ffn_kernel(*refs, with_attn, with_final_norm):
    refs = list(refs)
    x_ref = refs.pop(0)
    if with_attn:
        a_ref, wo_ref = refs.pop(0), refs.pop(0)
    g_ref, wup_ref, cw_ref, cb_ref, wdown_ref = refs[:5]
    refs = refs[5:]
    if with_final_norm:
        gf_ref = refs.pop(0)
    o_ref, r_scr, acc_scr = refs

    tm, d = x_ref.shape
    ff = wdown_ref.shape[0]
    fc = FF_CHUNK
    _load_carry(r_scr, tm)

    x = x_ref[...]
    if with_attn:
        x = x + _dot(a_ref[...], wo_ref[...])
    h = ((x * _rms_scale(x)) * g_ref[...]).astype(BF16)

    for c in range(ff // fc):
        for half in range(2):
            cols = slice(half * ff + c * fc, half * ff + (c + 1) * fc)
            r_scr[SUBLANES:SUBLANES + tm, cols] = _dot(h, wup_ref[:, cols])

    acc_scr[...] = x
    for c in range(ff // fc):
        gcols = slice(c * fc, (c + 1) * fc)
        vcols = slice(ff + c * fc, ff + (c + 1) * fc)
        ug = _causal_conv3(r_scr.at[:, gcols], cw_ref[:, gcols], tm) + cb_ref[:, gcols]
        uv = _causal_conv3(r_scr.at[:, vcols], cw_ref[:, vcols], tm) + cb_ref[:, vcols]
        act = (ug * jax.nn.sigmoid(ug) * uv).astype(BF16)
        acc_scr[...] += _dot(act, wdown_ref[gcols, :])

    y = acc_scr[...]
    if with_final_norm:
        y = (y * _rms_scale(y)) * gf_ref[...]
    o_ref[...] = y


def _ffn(x, g, w_up, conv_w, conv_b, w_down, attn=None, w_o=None, final_g=None):
    bsz, s, d = x.shape
    ff = w_down.shape[0]
    tm = TOKEN_TILE
    tile = pl.BlockSpec((None, tm, d), lambda b, i: (b, i, 0))
    args, specs = [x], [tile]
    if attn is not None:
        args += [attn, w_o]
        specs += [tile, _const_spec((d, d))]
    args += [g, w_up, conv_w, conv_b, w_down]
    specs += [_const_spec((1, d)), _const_spec((d, 2 * ff)), _const_spec((3, 2 * ff)),
              _const_spec((1, 2 * ff)), _const_spec((ff, d))]
    if final_g is not None:
        args.append(final_g)
        specs.append(_const_spec((1, d)))
    kernel = functools.partial(_ffn_kernel, with_attn=attn is not None,
                               with_final_norm=final_g is not None)
    return pl.pallas_call(
        kernel,
        grid=(bsz, s // tm),
        in_specs=specs,
        out_specs=tile,
        out_shape=jax.ShapeDtypeStruct(x.shape, F32),
        scratch_shapes=[pltpu.VMEM((tm + SUBLANES, 2 * ff), F32),
                        pltpu.VMEM((tm, d), F32)],
        compiler_params=_params(2),
        name="conv_glu_ffn",
    )(*args)


def _qkv_kernel(x_ref, gq_ref, gkv_ref, wq_ref, wk_ref, wvt_ref,
                q_ref, k_ref, vt_ref, km_ref):
    tm, d = x_ref.shape
    x = x_ref[...]
    xn = x * _rms_scale(x)
    hq = (xn * gq_ref[...]).astype(BF16)
    hkv = (xn * gkv_ref[...]).astype(BF16)
    q_ref[...] = (_dot(hq, wq_ref[...]) * (HEAD_DIM ** -0.5)).astype(BF16)
    k = _dot(hkv, wk_ref[...])
    k_ref[...] = k.astype(BF16)
    nblk = tm // MOBA_BLOCK
    km_ref[...] = jnp.sum(k.reshape(nblk, MOBA_BLOCK, d), axis=1) / float(MOBA_BLOCK)
    vt_ref[...] = _dot_nt(wvt_ref[...], hkv).astype(BF16)


def _qkv(x, gq, gkv, w_q, w_k, w_vt):
    bsz, s, d = x.shape
    tm = TOKEN_TILE
    nblk = tm // MOBA_BLOCK
    tile = pl.BlockSpec((None, tm, d), lambda b, i: (b, i, 0))
    return pl.pallas_call(
        _qkv_kernel,
        grid=(bsz, s // tm),
        in_specs=[tile, _const_spec((1, d)), _const_spec((1, d)),
                  _const_spec((d, d)), _const_spec((d, d)), _const_spec((d, d))],
        out_specs=[tile, tile,
                   pl.BlockSpec((None, d, tm), lambda b, i: (b, 0, i)),
                   pl.BlockSpec((None, None, nblk, d), lambda b, i: (b, i, 0, 0))],
        out_shape=[jax.ShapeDtypeStruct((bsz, s, d), BF16),
                   jax.ShapeDtypeStruct((bsz, s, d), BF16),
                   jax.ShapeDtypeStruct((bsz, d, s), BF16),
                   jax.ShapeDtypeStruct((bsz, s // tm, nblk, d), F32)],
        compiler_params=_params(2),
        name="qkv_proj",
    )(x, gq, gkv, w_q, w_k, w_vt)


def _bucket_starts():
    n = np.arange(0, 2 * MOBA_BLOCK)
    nf = np.maximum(n, 1).astype(np.float64)
    large = MAX_EXACT + (np.log(nf / MAX_EXACT) / math.log(REL_MAX_DIST / MAX_EXACT)
                         * (N_BUCKETS - MAX_EXACT)).astype(np.int64)
    bucket = np.where(n < MAX_EXACT, n, np.minimum(large, N_BUCKETS - 1))
    assert np.all(np.diff(bucket) >= 0) and bucket[REL_MAX_DIST] == N_BUCKETS - 1
    return [int(np.argmax(bucket >= b)) for b in range(N_BUCKETS)]


def _bias_kernel(rb_ref, o_ref):
    h = pl.program_id(0)
    starts = _bucket_starts()
    key = lax.broadcasted_iota(jnp.int32, (MOBA_BLOCK, MOBA_BLOCK), 0)
    qry = lax.broadcasted_iota(jnp.int32, (MOBA_BLOCK, MOBA_BLOCK), 1)
    for dblk in range(2):
        dist = qry - key + dblk * MOBA_BLOCK
        val = jnp.full(dist.shape, rb_ref[0, h], F32)
        for b in range(1, N_BUCKETS):
            val = jnp.where(dist >= starts[b], rb_ref[b, h], val)
        if dblk == 0:
            val = jnp.where(dist >= 0, val, NEG)
        o_ref[dblk] = val


def _bias_tiles(rel_bias):
    return pl.pallas_call(
        _bias_kernel,
        grid=(N_HEADS,),
        in_specs=[pl.BlockSpec(memory_space=pltpu.SMEM)],
        out_specs=pl.BlockSpec((None, 2, MOBA_BLOCK, MOBA_BLOCK), lambda h: (h, 0, 0, 0)),
        out_shape=jax.ShapeDtypeStruct((N_HEADS, 2, MOBA_BLOCK, MOBA_BLOCK), F32),
        compiler_params=_params(1),
        name="rel_bias_tiles",
    )(rel_bias)


def _moba_kernel(rb_ref, q_ref, k_ref, vt_ref, km_ref, bias_ref, o_ref, kh_scr, madd_scr):
    pair = pl.program_id(0)
    s = q_ref.shape[0]
    nb = s // MOBA_BLOCK
    blk = MOBA_BLOCK

    k = k_ref[...]
    lane = lax.broadcasted_iota(jnp.int32, k.shape, 1)
    kh_scr[0] = jnp.where(lane < HEAD_DIM, k, jnp.zeros_like(k))
    kh_scr[1] = jnp.where(lane >= HEAD_DIM, k, jnp.zeros_like(k))

    km = km_ref[...]
    lane_b = lax.broadcasted_iota(jnp.int32, km.shape, 1)
    km2 = jnp.concatenate([jnp.where(lane_b < HEAD_DIM, km, 0.0),
                           jnp.where(lane_b >= HEAD_DIM, km, 0.0)], axis=0).astype(BF16)
    gate_t = _dot_nt(km2, q_ref[...])
    blk_id = lax.broadcasted_iota(jnp.int32, (nb, s), 0)
    own = jnp.right_shift(lax.broadcasted_iota(jnp.int32, (nb, s), 1), int(math.log2(blk)))
    past = blk_id < own
    for h in range(HEADS_PER_STEP):
        g = jnp.where(past, gate_t[h * nb:(h + 1) * nb, :], NEG)
        beaten_by = jnp.zeros((nb, s), jnp.int32)
        for j in range(nb):
            row = g[j:j + 1, :]
            ahead = (row > g) | ((row == g) & (j < blk_id))
            beaten_by = beaten_by + ahead.astype(jnp.int32)
        chosen = past & (beaten_by < MOBA_TOP_K)
        madd_scr[h] = jnp.where(chosen, 0.0, NEG)

    for i in range(nb):
        qi = q_ref[i * blk:(i + 1) * blk, :]
        nk = (i + 1) * blk
        outs = []
        for h in range(HEADS_PER_STEP):
            far_bias = rb_ref[N_BUCKETS - 1, pair * HEADS_PER_STEP + h]
            s_t = _dot_nt(kh_scr[h, 0:nk, :], qi)
            parts = []
            for j in range(i + 1):
                part = s_t[j * blk:(j + 1) * blk, :]
                if j == i:
                    part = part + bias_ref[h, 0]
                else:
                    sel = madd_scr[h, j:j + 1, i * blk:(i + 1) * blk]
                    if j == i - 1:
                        part = part + (bias_ref[h, 1] + sel)
                    else:
                        part = part + (sel + far_bias)
                parts.append(part)
            s_t = parts[0] if len(parts) == 1 else jnp.concatenate(parts, axis=0)
            m = jnp.max(s_t, axis=0, keepdims=True)
            p_t = jnp.exp(s_t - m)
            denom = jnp.sum(p_t, axis=0, keepdims=True)
            o_t = _dot(vt_ref[h * HEAD_DIM:(h + 1) * HEAD_DIM, 0:nk], p_t.astype(BF16))
            outs.append(o_t / denom)
        o_ref[i * blk:(i + 1) * blk, :] = jnp.concatenate(outs, axis=0).T.astype(o_ref.dtype)


def _moba(rel_bias, q, k, vt, kmean, bias_tiles):
    bsz, s, d = q.shape
    nb = s // MOBA_BLOCK
    n_pairs = N_HEADS // HEADS_PER_STEP
    col = pl.BlockSpec((None, s, LANES), lambda p, b: (b, 0, p))
    return pl.pallas_call(
        _moba_kernel,
        grid=(n_pairs, bsz),
        in_specs=[pl.BlockSpec(memory_space=pltpu.SMEM),
                  col, col,
                  pl.BlockSpec((None, LANES, s), lambda p, b: (b, p, 0)),
                  pl.BlockSpec((None, nb, LANES), lambda p, b: (b, 0, p)),
                  pl.BlockSpec((HEADS_PER_STEP, 2, MOBA_BLOCK, MOBA_BLOCK),
                               lambda p, b: (p, 0, 0, 0))],
        out_specs=col,
        out_shape=jax.ShapeDtypeStruct((bsz, s, d), BF16),
        scratch_shapes=[pltpu.VMEM((HEADS_PER_STEP, s, LANES), BF16),
                        pltpu.VMEM((HEADS_PER_STEP, nb, s), F32)],
        compiler_params=_params(2),
        name="moba_attention",
    )(rel_bias, q, k, vt, kmean, bias_tiles)


def kernel(x, a_norm, a_w_in, a_conv, a_w_out, kv_norm, w_k, w_v, b_norm, b_w_q, b_w_o,
           rel_bias, f_norm, f_w_up, f_conv, f_conv_b, f_w_down, final_norm):
    bsz, s, d = x.shape
    assert (s % TOKEN_TILE == 0 and TOKEN_TILE % MOBA_BLOCK == 0 and d == D_MODEL
            and a_norm.shape[0] == 1 and b_norm.shape[0] == 1 and D_FF % FF_CHUNK == 0)
    row = lambda v: v.reshape(1, -1)
    bf = lambda w: w.astype(BF16)

    x = _mixer(x, row(a_norm[0]), bf(a_w_in[0]), a_conv[0], bf(a_w_out[0]))
    x = _ffn(x, row(f_norm[0]), bf(f_w_up[0]), f_conv[0], row(f_conv_b[0]), bf(f_w_down[0]))

    q, k, vt, kmean = _qkv(x, row(b_norm[0]), row(kv_norm), bf(b_w_q[0]), bf(w_k), bf(w_v.T))
    kmean = kmean.reshape(bsz, s // MOBA_BLOCK, d)
    attn = _moba(rel_bias, q, k, vt, kmean, _bias_tiles(rel_bias))

    return _ffn(x, row(f_norm[1]), bf(f_w_up[1]), f_conv[1], row(f_conv_b[1]),
                bf(f_w_down[1]), attn=attn, w_o=bf(b_w_o[0]), final_g=row(final_norm))
```

```python
import functools
import math

import numpy as np
import jax
import jax.numpy as jnp
from jax import lax
from jax.experimental import pallas as pl
from jax.experimental.pallas import tpu as pltpu

D_MODEL = 1024
N_HEADS = 16
HEAD_DIM = 64
MOBA_BLOCK = 256
MOBA_TOP_K = 3
N_BUCKETS = 32
MAX_EXACT = N_BUCKETS // 2
REL_MAX_DIST = 128
D_FF = 2816
EPS = 1e-6
NEG = -1e30
LOG2E = math.log2(math.e)

SUBLANES = 8
LANES = 128
HEADS_PER_STEP = LANES // HEAD_DIM
V_AUG_ROWS = HEAD_DIM + 16
VMEM_LIMIT_BYTES = 56 * 1024 * 1024

TOKEN_TILE = 512
FF_CHUNK = 256

BF16 = jnp.bfloat16
F32 = jnp.float32


def _dot(a, b):
    return jnp.dot(a, b, preferred_element_type=F32)


def _dot_nt(a, b):
    return lax.dot_general(a, b, (((1,), (1,)), ((), ())), preferred_element_type=F32)


def _rms_scale(x):
    return lax.rsqrt(jnp.mean(x * x, axis=-1, keepdims=True) + EPS)


def _const_spec(shape):
    return pl.BlockSpec(shape, lambda *_: (0,) * len(shape), pipeline_mode=pl.Buffered(1))


def _params(n_axes):
    return pltpu.CompilerParams(
        dimension_semantics=("arbitrary",) * n_axes,
        vmem_limit_bytes=VMEM_LIMIT_BYTES)


def _load_carry(scr, tm):
    i = pl.program_id(1)

    @pl.when(i == 0)
    def _():
        scr[0:SUBLANES, :] = jnp.zeros((SUBLANES, scr.shape[1]), F32)

    @pl.when(i > 0)
    def _():
        scr[0:SUBLANES, :] = scr[tm:tm + SUBLANES, :]


def _causal_conv3(scr, w, tm):
    return (w[0:1, :] * scr[SUBLANES - 2:SUBLANES - 2 + tm, :]
            + w[1:2, :] * scr[SUBLANES - 1:SUBLANES - 1 + tm, :]
            + w[2:3, :] * scr[SUBLANES:SUBLANES + tm, :])


def _mixer_kernel(x_ref, g_ref, win_ref, cw_ref, wout_ref, o_ref, z_scr):
    tm, d = x_ref.shape
    _load_carry(z_scr, tm)
    x = x_ref[...]
    h = ((x * _rms_scale(x)) * g_ref[...]).astype(BF16)
    b_gate = _dot(h, win_ref[:, 0:d])
    c_gate = _dot(h, win_ref[:, d:2 * d])
    hx = _dot(h, win_ref[:, 2 * d:3 * d])
    z_scr[SUBLANES:SUBLANES + tm, :] = c_gate * hx
    y = (b_gate * _causal_conv3(z_scr, cw_ref[...], tm)).astype(BF16)
    o_ref[...] = x + _dot(y, wout_ref[...])


def _mixer(x, g, w_in, conv_w, w_out):
    bsz, s, d = x.shape
    tm = TOKEN_TILE
    tile = pl.BlockSpec((None, tm, d), lambda b, i: (b, i, 0))
    return pl.pallas_call(
        _mixer_kernel,
        grid=(bsz, s // tm),
        in_specs=[tile, _const_spec((1, d)), _const_spec((d, 3 * d)),
                  _const_spec((3, d)), _const_spec((d, d))],
        out_specs=tile,
        out_shape=jax.ShapeDtypeStruct(x.shape, F32),
        scratch_shapes=[pltpu.VMEM((tm + SUBLANES, d), F32)],
        compiler_params=_params(2),
        name="short_conv_mixer",
    )(x, g, w_in, conv_w, w_out)


def _ffn_kernel(*refs, with_attn, with_final_norm):
    refs = list(refs)
    x_ref = refs.pop(0)
    if with_attn:
        a_ref, wo_ref = refs.pop(0), refs.pop(0)
    g_ref, wup_ref, cw_ref, cb_ref, wdown_ref = refs[:5]
    refs = refs[5:]
    if with_final_norm:
        gf_ref = refs.pop(0)
    o_ref, r_scr, acc_scr = refs

    tm, d = x_ref.shape
    ff = wdown_ref.shape[0]
    fc = FF_CHUNK
    _load_carry(r_scr, tm)

    x = x_ref[...]
    if with_attn:
        x = x + _dot(a_ref[...], wo_ref[...])
    h = ((x * _rms_scale(x)) * g_ref[...]).astype(BF16)

    for c in range(ff // fc):
        for half in range(2):
            cols = slice(half * ff + c * fc, half * ff + (c + 1) * fc)
            r_scr[SUBLANES:SUBLANES + tm, cols] = _dot(h, wup_ref[:, cols])

    acc_scr[...] = x
    for c in range(ff // fc):
        gcols = slice(c * fc, (c + 1) * fc)
        vcols = slice(ff + c * fc, ff + (c + 1) * fc)
        ug = _causal_conv3(r_scr.at[:, gcols], cw_ref[:, gcols], tm) + cb_ref[:, gcols]
        uv = _causal_conv3(r_scr.at[:, vcols], cw_ref[:, vcols], tm) + cb_ref[:, vcols]
        act = (ug * jax.nn.sigmoid(ug) * uv).astype(BF16)
        acc_scr[...] += _dot(act, wdown_ref[gcols, :])

    y = acc_scr[...]
    if with_final_norm:
        y = (y * _rms_scale(y)) * gf_ref[...]
    o_ref[...] = y


def _ffn(x, g, w_up, conv_w, conv_b, w_down, attn=None, w_o=None, final_g=None):
    bsz, s, d = x.shape
    ff = w_down.shape[0]
    tm = TOKEN_TILE
    tile = pl.BlockSpec((None, tm, d), lambda b, i: (b, i, 0))
    args, specs = [x], [tile]
    if attn is not None:
        args += [attn, w_o]
        specs += [tile, _const_spec((d, d))]
    args += [g, w_up, conv_w, conv_b, w_down]
    specs += [_const_spec((1, d)), _const_spec((d, 2 * ff)), _const_spec((3, 2 * ff)),
              _const_spec((1, 2 * ff)), _const_spec((ff, d))]
    if final_g is not None:
        args.append(final_g)
        specs.append(_const_spec((1, d)))
    kernel = functools.partial(_ffn_kernel, with_attn=attn is not None,
                               with_final_norm=final_g is not None)
    return pl.pallas_call(
        kernel,
        grid=(bsz, s // tm),
        in_specs=specs,
        out_specs=tile,
        out_shape=jax.ShapeDtypeStruct(x.shape, F32),
        scratch_shapes=[pltpu.VMEM((tm + SUBLANES, 2 * ff), F32),
                        pltpu.VMEM((tm, d), F32)],
        compiler_params=_params(2),
        name="conv_glu_ffn",
    )(*args)


def _qkv_kernel(x_ref, gq_ref, gkv_ref, wq_ref, wk_ref, wvt_ref,
                q_ref, k_ref, vt_ref, km_ref):
    tm, d = x_ref.shape
    x = x_ref[...]
    xn = x * _rms_scale(x)
    hq = (xn * gq_ref[...]).astype(BF16)
    hkv = (xn * gkv_ref[...]).astype(BF16)
    q_ref[...] = (_dot(hq, wq_ref[...]) * (HEAD_DIM ** -0.5 * LOG2E)).astype(BF16)
    k = _dot(hkv, wk_ref[...])
    k_ref[...] = k.astype(BF16)
    nblk = tm // MOBA_BLOCK
    km_ref[...] = jnp.sum(k.reshape(nblk, MOBA_BLOCK, d), axis=1) / float(MOBA_BLOCK)
    vt_ref[...] = _dot_nt(wvt_ref[...], hkv).astype(BF16)


def _qkv(x, gq, gkv, w_q, w_k, w_vt):
    bsz, s, d = x.shape
    tm = TOKEN_TILE
    nblk = tm // MOBA_BLOCK
    tile = pl.BlockSpec((None, tm, d), lambda b, i: (b, i, 0))
    return pl.pallas_call(
        _qkv_kernel,
        grid=(bsz, s // tm),
        in_specs=[tile, _const_spec((1, d)), _const_spec((1, d)),
                  _const_spec((d, d)), _const_spec((d, d)), _const_spec((d, d))],
        out_specs=[tile, tile,
                   pl.BlockSpec((None, d, tm), lambda b, i: (b, 0, i)),
                   pl.BlockSpec((None, None, nblk, d), lambda b, i: (b, i, 0, 0))],
        out_shape=[jax.ShapeDtypeStruct((bsz, s, d), BF16),
                   jax.ShapeDtypeStruct((bsz, s, d), BF16),
                   jax.ShapeDtypeStruct((bsz, d, s), BF16),
                   jax.ShapeDtypeStruct((bsz, s // tm, nblk, d), F32)],
        compiler_params=_params(2),
        name="qkv_proj",
    )(x, gq, gkv, w_q, w_k, w_vt)


def _bucket_starts():
    n = np.arange(0, 2 * MOBA_BLOCK)
    nf = np.maximum(n, 1).astype(np.float64)
    large = MAX_EXACT + (np.log(nf / MAX_EXACT) / math.log(REL_MAX_DIST / MAX_EXACT)
                         * (N_BUCKETS - MAX_EXACT)).astype(np.int64)
    bucket = np.where(n < MAX_EXACT, n, np.minimum(large, N_BUCKETS - 1))
    assert np.all(np.diff(bucket) >= 0) and bucket[REL_MAX_DIST] == N_BUCKETS - 1
    return [int(np.argmax(bucket >= b)) for b in range(N_BUCKETS)]


def _bias_kernel(rb_ref, o_ref):
    h = pl.program_id(0)
    starts = _bucket_starts()
    key = lax.broadcasted_iota(jnp.int32, (MOBA_BLOCK, MOBA_BLOCK), 0)
    qry = lax.broadcasted_iota(jnp.int32, (MOBA_BLOCK, MOBA_BLOCK), 1)
    for dblk in range(2):
        dist = qry - key + dblk * MOBA_BLOCK
        val = jnp.full(dist.shape, rb_ref[0, h], F32)
        for b in range(1, N_BUCKETS):
            val = jnp.where(dist >= starts[b], rb_ref[b, h], val)
        if dblk == 0:
            val = jnp.where(dist >= 0, val, NEG)
        o_ref[dblk] = val * LOG2E


def _bias_tiles(rel_bias):
    return pl.pallas_call(
        _bias_kernel,
        grid=(N_HEADS,),
        in_specs=[pl.BlockSpec(memory_space=pltpu.SMEM)],
        out_specs=pl.BlockSpec((None, 2, MOBA_BLOCK, MOBA_BLOCK), lambda h: (h, 0, 0, 0)),
        out_shape=jax.ShapeDtypeStruct((N_HEADS, 2, MOBA_BLOCK, MOBA_BLOCK), F32),
        compiler_params=_params(1),
        name="rel_bias_tiles",
    )(rel_bias)


def _moba_kernel(rb_ref, q_ref, k_ref, vt_ref, km_ref, bias_ref, o_ref,
                 kh_scr, va_scr, madd_scr, s_scr, p_scr):
    pair = pl.program_id(0)
    s = q_ref.shape[0]
    nb = s // MOBA_BLOCK
    blk = MOBA_BLOCK
    units = [(i, h) for i in range(nb) for h in range(HEADS_PER_STEP)]

    k = k_ref[...]
    lane = lax.broadcasted_iota(jnp.int32, k.shape, 1)
    kh_scr[0] = jnp.where(lane < HEAD_DIM, k, jnp.zeros_like(k))
    kh_scr[1] = jnp.where(lane >= HEAD_DIM, k, jnp.zeros_like(k))

    ones_row = (lax.broadcasted_iota(jnp.int32, (V_AUG_ROWS - HEAD_DIM, s), 0) == 0).astype(BF16)
    for h in range(HEADS_PER_STEP):
        va_scr[h, 0:HEAD_DIM, :] = vt_ref[h * HEAD_DIM:(h + 1) * HEAD_DIM, :]
        va_scr[h, HEAD_DIM:V_AUG_ROWS, :] = ones_row

    km = km_ref[...]
    lane_b = lax.broadcasted_iota(jnp.int32, km.shape, 1)
    km2 = jnp.concatenate([jnp.where(lane_b < HEAD_DIM, km, 0.0),
                           jnp.where(lane_b >= HEAD_DIM, km, 0.0)], axis=0).astype(BF16)
    gate_t = _dot_nt(km2, q_ref[...])
    blk_id = lax.broadcasted_iota(jnp.int32, (nb, s), 0)
    own = jnp.right_shift(lax.broadcasted_iota(jnp.int32, (nb, s), 1), int(math.log2(blk)))
    past = blk_id < own
    for h in range(HEADS_PER_STEP):
        g = jnp.where(past, gate_t[h * nb:(h + 1) * nb, :], NEG)
        beaten_by = jnp.zeros((nb, s), jnp.int32)
        for j in range(nb):
            row = g[j:j + 1, :]
            ahead = (row > g) | ((row == g) & (j < blk_id))
            beaten_by = beaten_by + ahead.astype(jnp.int32)
        chosen = past & (beaten_by < MOBA_TOP_K)
        madd_scr[h] = jnp.where(chosen, 0.0, NEG)

    far_bias = [rb_ref[N_BUCKETS - 1, pair * HEADS_PER_STEP + h] * LOG2E
                for h in range(HEADS_PER_STEP)]

    def scores(u, j):
        i, h = units[u]
        rows = slice(j * blk, (j + 1) * blk)
        part = _dot_nt(kh_scr[h, rows, :], q_ref[i * blk:(i + 1) * blk, :])
        if j == i:
            part = part + bias_ref[h, 0]
        else:
            sel = madd_scr[h, j:j + 1, i * blk:(i + 1) * blk]
            if j == i - 1:
                part = part + (bias_ref[h, 1] + sel)
            else:
                part = part + (sel + far_bias[h])
        s_scr[u % 2, rows, :] = part
        return jnp.max(part.reshape(blk // SUBLANES, SUBLANES, blk), axis=0)

    def probs(u, j, m):
        rows = slice(j * blk, (j + 1) * blk)
        p_scr[u % 2, rows, :] = jnp.exp2(s_scr[u % 2, rows, :] - m).astype(BF16)

    def weighted_values(u):
        i, h = units[u]
        nk = (i + 1) * blk
        o_aug = _dot(va_scr[h, :, 0:nk], p_scr[u % 2, 0:nk, :])
        return o_aug[0:HEAD_DIM, :] / o_aug[HEAD_DIM:HEAD_DIM + 1, :]

    col_max = [None] * len(units)
    outs = []
    for step in range(len(units) + 1):
        n_new = units[step][0] + 1 if step < len(units) else 0
        n_old = units[step - 1][0] + 1 if step >= 1 else 0
        m8 = None
        for j in range(max(n_new, n_old)):
            if j < n_new:
                pm = scores(step, j)
                m8 = pm if m8 is None else jnp.maximum(m8, pm)
            if j < n_old:
                probs(step - 1, j, col_max[step - 1])
        if n_new:
            col_max[step] = jnp.max(m8, axis=0, keepdims=True)
        if n_old:
            outs.append(weighted_values(step - 1))
            if len(outs) == HEADS_PER_STEP:
                i = units[step - 1][0]
                o_ref[i * blk:(i + 1) * blk, :] = (
                    jnp.concatenate(outs, axis=0).T.astype(o_ref.dtype))
                outs = []


def _moba(rel_bias, q, k, vt, kmean, bias_tiles):
    bsz, s, d = q.shape
    nb = s // MOBA_BLOCK
    n_pairs = N_HEADS // HEADS_PER_STEP
    col = pl.BlockSpec((None, s, LANES), lambda p, b: (b, 0, p))
    return pl.pallas_call(
        _moba_kernel,
        grid=(n_pairs, bsz),
        in_specs=[pl.BlockSpec(memory_space=pltpu.SMEM),
                  col, col,
                  pl.BlockSpec((None, LANES, s), lambda p, b: (b, p, 0)),
                  pl.BlockSpec((None, nb, LANES), lambda p, b: (b, 0, p)),
                  pl.BlockSpec((HEADS_PER_STEP, 2, MOBA_BLOCK, MOBA_BLOCK),
                               lambda p, b: (p, 0, 0, 0))],
        out_specs=col,
        out_shape=jax.ShapeDtypeStruct((bsz, s, d), BF16),
        scratch_shapes=[pltpu.VMEM((HEADS_PER_STEP, s, LANES), BF16),
                        pltpu.VMEM((HEADS_PER_STEP, V_AUG_ROWS, s), BF16),
                        pltpu.VMEM((HEADS_PER_STEP, nb, s), F32),
                        pltpu.VMEM((2, s, MOBA_BLOCK), F32),
                        pltpu.VMEM((2, s, MOBA_BLOCK), BF16)],
        compiler_params=_params(2),
        name="moba_attention",
    )(rel_bias, q, k, vt, kmean, bias_tiles)


def kernel(x, a_norm, a_w_in, a_conv, a_w_out, kv_norm, w_k, w_v, b_norm, b_w_q, b_w_o,
           rel_bias, f_norm, f_w_up, f_conv, f_conv_b, f_w_down, final_norm):
    bsz, s, d = x.shape
    assert (s % TOKEN_TILE == 0 and TOKEN_TILE % MOBA_BLOCK == 0 and d == D_MODEL
            and a_norm.shape[0] == 1 and b_norm.shape[0] == 1 and D_FF % FF_CHUNK == 0)
    row = lambda v: v.reshape(1, -1)
    bf = lambda w: w.astype(BF16)

    x = _mixer(x, row(a_norm[0]), bf(a_w_in[0]), a_conv[0], bf(a_w_out[0]))
    x = _ffn(x, row(f_norm[0]), bf(f_w_up[0]), f_conv[0], row(f_conv_b[0]), bf(f_w_down[0]))

    q, k, vt, kmean = _qkv(x, row(b_norm[0]), row(kv_norm), bf(b_w_q[0]), bf(w_k), bf(w_v.T))
    kmean = kmean.reshape(bsz, s // MOBA_BLOCK, d)
    attn = _moba(rel_bias, q, k, vt, kmean, _bias_tiles(rel_bias))

    return _ffn(x, row(f_norm[1]), bf(f_w_up[1]), f_conv[1], row(f_conv_b[1]),
                bf(f_w_down[1]), attn=attn, w_o=bf(b_w_o[0]), final_g=row(final_norm))
```

```python
import functools
import math

import numpy as np
import jax
import jax.numpy as jnp
from jax import lax
from jax.experimental import pallas as pl
from jax.experimental.pallas import tpu as pltpu

D_MODEL = 1024
N_HEADS = 16
HEAD_DIM = 64
MOBA_BLOCK = 256
MOBA_TOP_K = 3
N_BUCKETS = 32
MAX_EXACT = N_BUCKETS // 2
REL_MAX_DIST = 128
D_FF = 2816
EPS = 1e-6
NEG = -1e30
LOG2E = math.log2(math.e)

SUBLANES = 8
LANES = 128
HEADS_PER_LANE_BLOCK = LANES // HEAD_DIM
MOBA_HEADS_PER_STEP = 4
V_AUG_ROWS = HEAD_DIM + 16
VMEM_LIMIT_BYTES = 56 * 1024 * 1024

TOKEN_TILE = 1024
FFN_TOKEN_TILE = 512
FF_CHUNK = 256

BF16 = jnp.bfloat16
F32 = jnp.float32


def _dot(a, b):
    return jnp.dot(a, b, preferred_element_type=F32)


def _dot_nt(a, b):
    return lax.dot_general(a, b, (((1,), (1,)), ((), ())), preferred_element_type=F32)


def _rms_scale(x):
    return lax.rsqrt(jnp.mean(x * x, axis=-1, keepdims=True) + EPS)


def _const_spec(shape):
    return pl.BlockSpec(shape, lambda *_: (0,) * len(shape), pipeline_mode=pl.Buffered(1))


def _params(n_axes):
    return pltpu.CompilerParams(
        dimension_semantics=("arbitrary",) * n_axes,
        vmem_limit_bytes=VMEM_LIMIT_BYTES)


def _load_carry(scr, tm):
    i = pl.program_id(1)

    @pl.when(i == 0)
    def _():
        scr[0:SUBLANES, :] = jnp.zeros((SUBLANES, scr.shape[1]), F32)

    @pl.when(i > 0)
    def _():
        scr[0:SUBLANES, :] = scr[tm:tm + SUBLANES, :]


def _causal_conv3(scr, w, r0, nrows):
    base = SUBLANES + r0
    return (w[0:1, :] * scr[base - 2:base - 2 + nrows, :]
            + w[1:2, :] * scr[base - 1:base - 1 + nrows, :]
            + w[2:3, :] * scr[base:base + nrows, :])


def _mixer_kernel(x_ref, g_ref, win_ref, cw_ref, wout_ref, o_ref, z_scr):
    tm, d = x_ref.shape
    _load_carry(z_scr, tm)
    x = x_ref[...]
    h = ((x * _rms_scale(x)) * g_ref[...]).astype(BF16)
    b_gate = _dot(h, win_ref[:, 0:d])
    c_gate = _dot(h, win_ref[:, d:2 * d])
    hx = _dot(h, win_ref[:, 2 * d:3 * d])
    z_scr[SUBLANES:SUBLANES + tm, :] = c_gate * hx
    y = (b_gate * _causal_conv3(z_scr, cw_ref[...], 0, tm)).astype(BF16)
    o_ref[...] = x + _dot(y, wout_ref[...])


def _mixer(x, g, w_in, conv_w, w_out):
    bsz, s, d = x.shape
    tm = TOKEN_TILE
    tile = pl.BlockSpec((None, tm, d), lambda b, i: (b, i, 0))
    return pl.pallas_call(
        _mixer_kernel,
        grid=(bsz, s // tm),
        in_specs=[tile, _const_spec((1, d)), _const_spec((d, 3 * d)),
                  _const_spec((3, d)), _const_spec((d, d))],
        out_specs=tile,
        out_shape=jax.ShapeDtypeStruct(x.shape, F32),
        scratch_shapes=[pltpu.VMEM((tm + SUBLANES, d), F32)],
        compiler_params=_params(2),
        name="short_conv_mixer",
    )(x, g, w_in, conv_w, w_out)


def _ffn_kernel(*refs, with_attn, with_final_norm):
    refs = list(refs)
    x_ref = refs.pop(0)
    if with_attn:
        a_ref, wo_ref = refs.pop(0), refs.pop(0)
    g_ref, wup_ref, cw_ref, cb_ref, wdown_ref = refs[:5]
    refs = refs[5:]
    if with_final_norm:
        gf_ref = refs.pop(0)
    o_ref, r_scr, acc_scr = refs

    tm, d = x_ref.shape
    ff = wdown_ref.shape[0]
    fc = FF_CHUNK
    _load_carry(r_scr, tm)

    x = x_ref[...]
    if with_attn:
        x = x + _dot(a_ref[...], wo_ref[...])
    h = ((x * _rms_scale(x)) * g_ref[...]).astype(BF16)

    for c in range(ff // fc):
        for half in range(2):
            cols = slice(half * ff + c * fc, half * ff + (c + 1) * fc)
            r_scr[SUBLANES:SUBLANES + tm, cols] = _dot(h, wup_ref[:, cols])

    acc_scr[...] = x
    for c in range(ff // fc):
        gcols = slice(c * fc, (c + 1) * fc)
        vcols = slice(ff + c * fc, ff + (c + 1) * fc)
        ug = _causal_conv3(r_scr.at[:, gcols], cw_ref[:, gcols], 0, tm) + cb_ref[:, gcols]
        uv = _causal_conv3(r_scr.at[:, vcols], cw_ref[:, vcols], 0, tm) + cb_ref[:, vcols]
        act = (ug * jax.nn.sigmoid(ug) * uv).astype(BF16)
        acc_scr[...] += _dot(act, wdown_ref[gcols, :])

    y = acc_scr[...]
    if with_final_norm:
        y = (y * _rms_scale(y)) * gf_ref[...]
    o_ref[...] = y


def _ffn(x, g, w_up, conv_w, conv_b, w_down, attn=None, w_o=None, final_g=None):
    bsz, s, d = x.shape
    ff = w_down.shape[0]
    tm = FFN_TOKEN_TILE
    tile = pl.BlockSpec((None, tm, d), lambda b, i: (b, i, 0))
    args, specs = [x], [tile]
    if attn is not None:
        args += [attn, w_o]
        specs += [tile, _const_spec((d, d))]
    args += [g, w_up, conv_w, conv_b, w_down]
    specs += [_const_spec((1, d)), _const_spec((d, 2 * ff)), _const_spec((3, 2 * ff)),
              _const_spec((1, 2 * ff)), _const_spec((ff, d))]
    if final_g is not None:
        args.append(final_g)
        specs.append(_const_spec((1, d)))
    kernel = functools.partial(_ffn_kernel, with_attn=attn is not None,
                               with_final_norm=final_g is not None)
    return pl.pallas_call(
        kernel,
        grid=(bsz, s // tm),
        in_specs=specs,
        out_specs=tile,
        out_shape=jax.ShapeDtypeStruct(x.shape, F32),
        scratch_shapes=[pltpu.VMEM((tm + SUBLANES, 2 * ff), F32),
                        pltpu.VMEM((tm, d), F32)],
        compiler_params=_params(2),
        name="conv_glu_ffn",
    )(*args)


def _qkv_kernel(x_ref, gq_ref, gkv_ref, wq_ref, wk_ref, wvt_ref,
                q_ref, k_ref, vt_ref, km_ref):
    tm, d = x_ref.shape
    x = x_ref[...]
    xn = x * _rms_scale(x)
    hq = (xn * gq_ref[...]).astype(BF16)
    hkv = (xn * gkv_ref[...]).astype(BF16)
    q_ref[...] = (_dot(hq, wq_ref[...]) * (HEAD_DIM ** -0.5 * LOG2E)).astype(BF16)
    k = _dot(hkv, wk_ref[...])
    k_ref[...] = k.astype(BF16)
    nblk = tm // MOBA_BLOCK
    km_ref[...] = jnp.sum(k.reshape(nblk, MOBA_BLOCK, d), axis=1) / float(MOBA_BLOCK)
    vt_ref[...] = _dot_nt(wvt_ref[...], hkv).astype(BF16)


def _qkv(x, gq, gkv, w_q, w_k, w_vt):
    bsz, s, d = x.shape
    tm = TOKEN_TILE
    nblk = tm // MOBA_BLOCK
    tile = pl.BlockSpec((None, tm, d), lambda b, i: (b, i, 0))
    return pl.pallas_call(
        _qkv_kernel,
        grid=(bsz, s // tm),
        in_specs=[tile, _const_spec((1, d)), _const_spec((1, d)),
                  _const_spec((d, d)), _const_spec((d, d)), _const_spec((d, d))],
        out_specs=[tile, tile,
                   pl.BlockSpec((None, d, tm), lambda b, i: (b, 0, i)),
                   pl.BlockSpec((None, None, nblk, d), lambda b, i: (b, i, 0, 0))],
        out_shape=[jax.ShapeDtypeStruct((bsz, s, d), BF16),
                   jax.ShapeDtypeStruct((bsz, s, d), BF16),
                   jax.ShapeDtypeStruct((bsz, d, s), BF16),
                   jax.ShapeDtypeStruct((bsz, s // tm, nblk, d), F32)],
        compiler_params=_params(2),
        name="qkv_proj",
    )(x, gq, gkv, w_q, w_k, w_vt)


def _bucket_starts():
    n = np.arange(0, 2 * MOBA_BLOCK)
    nf = np.maximum(n, 1).astype(np.float64)
    large = MAX_EXACT + (np.log(nf / MAX_EXACT) / math.log(REL_MAX_DIST / MAX_EXACT)
                         * (N_BUCKETS - MAX_EXACT)).astype(np.int64)
    bucket = np.where(n < MAX_EXACT, n, np.minimum(large, N_BUCKETS - 1))
    assert np.all(np.diff(bucket) >= 0) and bucket[REL_MAX_DIST] == N_BUCKETS - 1
    return [int(np.argmax(bucket >= b)) for b in range(N_BUCKETS)]


def _bias_kernel(rb_ref, o_ref):
    h = pl.program_id(0)
    starts = _bucket_starts()
    key = lax.broadcasted_iota(jnp.int32, (MOBA_BLOCK, MOBA_BLOCK), 0)
    qry = lax.broadcasted_iota(jnp.int32, (MOBA_BLOCK, MOBA_BLOCK), 1)
    for dblk in range(2):
        dist = qry - key + dblk * MOBA_BLOCK
        val = jnp.full(dist.shape, rb_ref[0, h], F32)
        for b in range(1, N_BUCKETS):
            val = jnp.where(dist >= starts[b], rb_ref[b, h], val)
        if dblk == 0:
            val = jnp.where(dist >= 0, val, NEG)
        o_ref[dblk] = val * LOG2E


def _bias_tiles(rel_bias):
    return pl.pallas_call(
        _bias_kernel,
        grid=(N_HEADS,),
        in_specs=[pl.BlockSpec(memory_space=pltpu.SMEM)],
        out_specs=pl.BlockSpec((None, 2, MOBA_BLOCK, MOBA_BLOCK), lambda h: (h, 0, 0, 0)),
        out_shape=jax.ShapeDtypeStruct((N_HEADS, 2, MOBA_BLOCK, MOBA_BLOCK), F32),
        compiler_params=_params(1),
        name="rel_bias_tiles",
    )(rel_bias)


def _moba_kernel(rb_ref, q_ref, k_ref, vt_ref, km_ref, bias_ref, o_ref,
                 kh_scr, va_scr, madd_scr, s_scr, p_scr):
    s = q_ref.shape[0]
    nb = s // MOBA_BLOCK
    blk = MOBA_BLOCK
    n_heads = va_scr.shape[0]
    first_head = pl.program_id(0) * n_heads
    units = [(i, h) for i in reversed(range(nb)) for h in range(n_heads)]
    lanes_of = lambda h: slice((h // HEADS_PER_LANE_BLOCK) * LANES,
                               (h // HEADS_PER_LANE_BLOCK + 1) * LANES)

    blk_id = lax.broadcasted_iota(jnp.int32, (nb, s), 0)
    own = jnp.right_shift(lax.broadcasted_iota(jnp.int32, (nb, s), 1), int(math.log2(blk)))
    past = blk_id < own
    ones_row = (lax.broadcasted_iota(jnp.int32, (V_AUG_ROWS - HEAD_DIM, s), 0) == 0).astype(BF16)
    for p in range(n_heads // HEADS_PER_LANE_BLOCK):
        cols = slice(p * LANES, (p + 1) * LANES)
        k = k_ref[:, cols]
        lane = lax.broadcasted_iota(jnp.int32, k.shape, 1)
        km = km_ref[:, cols]
        lane_b = lax.broadcasted_iota(jnp.int32, km.shape, 1)
        km_heads = []
        for half in range(HEADS_PER_LANE_BLOCK):
            h = p * HEADS_PER_LANE_BLOCK + half
            lo, hi = half * HEAD_DIM, (half + 1) * HEAD_DIM
            kh_scr[h] = jnp.where((lane >= lo) & (lane < hi), k, jnp.zeros_like(k))
            km_heads.append(jnp.where((lane_b >= lo) & (lane_b < hi), km, 0.0))
            va_scr[h, 0:HEAD_DIM, :] = vt_ref[h * HEAD_DIM:(h + 1) * HEAD_DIM, :]
            va_scr[h, HEAD_DIM:V_AUG_ROWS, :] = ones_row

        gate_t = _dot_nt(jnp.concatenate(km_heads, axis=0).astype(BF16), q_ref[:, cols])
        for half in range(HEADS_PER_LANE_BLOCK):
            g = jnp.where(past, gate_t[half * nb:(half + 1) * nb, :], NEG)
            beaten_by = jnp.zeros((nb, s), jnp.int32)
            for j in range(nb):
                row = g[j:j + 1, :]
                ahead = (row > g) | ((row == g) & (j < blk_id))
                beaten_by = beaten_by + ahead.astype(jnp.int32)
            chosen = past & (beaten_by < MOBA_TOP_K)
            madd_scr[p * HEADS_PER_LANE_BLOCK + half] = jnp.where(chosen, 0.0, NEG)

    far_bias = [rb_ref[N_BUCKETS - 1, first_head + h] * LOG2E for h in range(n_heads)]

    def block_offset(u, j):
        i, h = units[u]
        sel = madd_scr[h, j:j + 1, i * blk:(i + 1) * blk]
        return sel if j == i - 1 else sel + far_bias[h]

    def scores(u, j):
        i, h = units[u]
        rows = slice(j * blk, (j + 1) * blk)
        part = _dot_nt(kh_scr[h, rows, :], q_ref[i * blk:(i + 1) * blk, lanes_of(h)])
        if j >= i - 1:
            part = part + bias_ref[h, i - j]
        s_scr[u % 2, rows, :] = part
        pm = jnp.max(part.reshape(blk // SUBLANES, SUBLANES, blk), axis=0)
        return pm if j == i else pm + block_offset(u, j)

    def probs(u, j, m):
        i, _ = units[u]
        rows = slice(j * blk, (j + 1) * blk)
        shift = m if j == i else m - block_offset(u, j)
        p_scr[u % 2, rows, :] = jnp.exp2(s_scr[u % 2, rows, :] - shift).astype(BF16)

    def weighted_values(u):
        i, h = units[u]
        nk = (i + 1) * blk
        o_aug = _dot(va_scr[h, :, 0:nk], p_scr[u % 2, 0:nk, :])
        return o_aug[0:HEAD_DIM, :] / o_aug[HEAD_DIM:HEAD_DIM + 1, :]

    col_max = [None] * len(units)
    outs = []
    for step in range(len(units) + 2):
        if step >= 2:
            outs.append(weighted_values(step - 2))
            if len(outs) == HEADS_PER_LANE_BLOCK:
                i, h = units[step - 2]
                o_ref[i * blk:(i + 1) * blk, lanes_of(h)] = (
                    jnp.concatenate(outs, axis=0).T.astype(o_ref.dtype))
                outs = []
        n_new = units[step][0] + 1 if step < len(units) else 0
        n_old = units[step - 1][0] + 1 if 1 <= step <= len(units) else 0
        m8 = None
        for j in range(max(n_new, n_old)):
            if j < n_new:
                pm = scores(step, j)
                m8 = pm if m8 is None else jnp.maximum(m8, pm)
            if j < n_old:
                probs(step - 1, j, col_max[step - 1])
        if n_new:
            col_max[step] = jnp.max(m8, axis=0, keepdims=True)


def _moba(rel_bias, q, k, vt, kmean, bias_tiles):
    bsz, s, d = q.shape
    nb = s // MOBA_BLOCK
    hs = MOBA_HEADS_PER_STEP
    width = hs * HEAD_DIM
    col = pl.BlockSpec((None, s, width), lambda p, b: (b, 0, p))
    return pl.pallas_call(
        _moba_kernel,
        grid=(N_HEADS // hs, bsz),
        in_specs=[pl.BlockSpec(memory_space=pltpu.SMEM),
                  col, col,
                  pl.BlockSpec((None, width, s), lambda p, b: (b, p, 0)),
                  pl.BlockSpec((None, nb, width), lambda p, b: (b, 0, p)),
                  pl.BlockSpec((hs, 2, MOBA_BLOCK, MOBA_BLOCK), lambda p, b: (p, 0, 0, 0))],
        out_specs=col,
        out_shape=jax.ShapeDtypeStruct((bsz, s, d), BF16),
        scratch_shapes=[pltpu.VMEM((hs, s, LANES), BF16),
                        pltpu.VMEM((hs, V_AUG_ROWS, s), BF16),
                        pltpu.VMEM((hs, nb, s), F32),
                        pltpu.VMEM((2, s, MOBA_BLOCK), F32),
                        pltpu.VMEM((2, s, MOBA_BLOCK), BF16)],
        compiler_params=_params(2),
        name="moba_attention",
    )(rel_bias, q, k, vt, kmean, bias_tiles)


def kernel(x, a_norm, a_w_in, a_conv, a_w_out, kv_norm, w_k, w_v, b_norm, b_w_q, b_w_o,
           rel_bias, f_norm, f_w_up, f_conv, f_conv_b, f_w_down, final_norm):
    bsz, s, d = x.shape
    assert (s % TOKEN_TILE == 0 and s % FFN_TOKEN_TILE == 0 and TOKEN_TILE % MOBA_BLOCK == 0
            and d == D_MODEL and N_HEADS % MOBA_HEADS_PER_STEP == 0
            and MOBA_HEADS_PER_STEP % HEADS_PER_LANE_BLOCK == 0
            and a_norm.shape[0] == 1 and b_norm.shape[0] == 1 and D_FF % FF_CHUNK == 0)
    row = lambda v: v.reshape(1, -1)
    bf = lambda w: w.astype(BF16)

    x = _mixer(x, row(a_norm[0]), bf(a_w_in[0]), a_conv[0], bf(a_w_out[0]))
    x = _ffn(x, row(f_norm[0]), bf(f_w_up[0]), f_conv[0], row(f_conv_b[0]), bf(f_w_down[0]))

    q, k, vt, kmean = _qkv(x, row(b_norm[0]), row(kv_norm), bf(b_w_q[0]), bf(w_k), bf(w_v.T))
    kmean = kmean.reshape(bsz, s // MOBA_BLOCK, d)
    attn = _moba(rel_bias, q, k, vt, kmean, _bias_tiles(rel_bias))

    return _ffn(x, row(f_norm[1]), bf(f_w_up[1]), f_conv[1], row(f_conv_b[1]),
                bf(f_w_down[1]), attn=attn, w_o=bf(b_w_o[0]), final_g=row(final_norm))
```

```python
import functools
import math

import numpy as np
import jax
import jax.numpy as jnp
from jax import lax
from jax.experimental import pallas as pl
from jax.experimental.pallas import tpu as pltpu

D_MODEL = 1024
N_HEADS = 16
HEAD_DIM = 64
MOBA_BLOCK = 256
MOBA_TOP_K = 3
N_BUCKETS = 32
MAX_EXACT = N_BUCKETS // 2
REL_MAX_DIST = 128
D_FF = 2816
EPS = 1e-6
NEG = -1e30
LOG2E = math.log2(math.e)

SUBLANES = 8
LANES = 128
HEADS_PER_LANE_BLOCK = LANES // HEAD_DIM
MOBA_HEADS_PER_STEP = 4
V_AUG_ROWS = HEAD_DIM + 16
VMEM_LIMIT_BYTES = 56 * 1024 * 1024

TOKEN_TILE = 1024
FFN_TOKEN_TILE = 512
FF_CHUNK = 256

BF16 = jnp.bfloat16
F32 = jnp.float32


def _dot(a, b):
    return jnp.dot(a, b, preferred_element_type=F32)


def _dot_nt(a, b):
    return lax.dot_general(a, b, (((1,), (1,)), ((), ())), preferred_element_type=F32)


def _rms_scale(x):
    return lax.rsqrt(jnp.mean(x * x, axis=-1, keepdims=True) + EPS)


def _const_spec(shape):
    return pl.BlockSpec(shape, lambda *_: (0,) * len(shape), pipeline_mode=pl.Buffered(1))


def _params(n_axes):
    return pltpu.CompilerParams(
        dimension_semantics=("arbitrary",) * n_axes,
        vmem_limit_bytes=VMEM_LIMIT_BYTES)


def _load_carry(scr, tm):
    i = pl.program_id(1)

    @pl.when(i == 0)
    def _():
        scr[0:SUBLANES, :] = jnp.zeros((SUBLANES, scr.shape[1]), F32)

    @pl.when(i > 0)
    def _():
        scr[0:SUBLANES, :] = scr[tm:tm + SUBLANES, :]


def _causal_conv3(scr, w, r0, nrows):
    base = SUBLANES + r0
    return (w[0:1, :] * scr[base - 2:base - 2 + nrows, :]
            + w[1:2, :] * scr[base - 1:base - 1 + nrows, :]
            + w[2:3, :] * scr[base:base + nrows, :])


def _cast_io(w, layer, grid):
    rows, cols = w.shape[-2:]
    n_steps = grid[0] * grid[1]
    bf16_rows = 2 * SUBLANES
    n_blocks = max(n for n in range(1, n_steps + 1)
                   if rows % n == 0 and (rows // n) % bf16_rows == 0)
    block_rows = rows // n_blocks

    def index(b, i):
        return jnp.minimum(b * grid[1] + i, n_blocks - 1), 0

    if w.ndim == 3:
        in_spec = pl.BlockSpec((None, block_rows, cols), lambda b, i: (layer,) + index(b, i))
    else:
        in_spec = pl.BlockSpec((block_rows, cols), index)
    return in_spec, pl.BlockSpec((block_rows, cols), index), jax.ShapeDtypeStruct((rows, cols), BF16)


def _cast_blocks(src_refs, dst_refs):
    for src, dst in zip(src_refs, dst_refs):
        dst[...] = src[...].astype(BF16)


def _mixer_kernel(x_ref, g_ref, win_ref, cw_ref, wout_ref, *rest):
    n_cast = (len(rest) - 2) // 2
    o_ref, z_scr = rest[n_cast], rest[-1]
    _cast_blocks(rest[:n_cast], rest[n_cast + 1:-1])
    tm, d = x_ref.shape
    _load_carry(z_scr, tm)
    x = x_ref[...]
    h = ((x * _rms_scale(x)) * g_ref[...]).astype(BF16)
    b_gate = _dot(h, win_ref[:, 0:d])
    c_gate = _dot(h, win_ref[:, d:2 * d])
    hx = _dot(h, win_ref[:, 2 * d:3 * d])
    z_scr[SUBLANES:SUBLANES + tm, :] = c_gate * hx
    y = (b_gate * _causal_conv3(z_scr, cw_ref[...], 0, tm)).astype(BF16)
    o_ref[...] = x + _dot(y, wout_ref[...])


def _mixer(x, g, w_in, conv_w, w_out, casts=()):
    bsz, s, d = x.shape
    tm = TOKEN_TILE
    grid = (bsz, s // tm)
    tile = pl.BlockSpec((None, tm, d), lambda b, i: (b, i, 0))
    cast_io = [_cast_io(w, layer, grid) for w, layer in casts]
    out = pl.pallas_call(
        _mixer_kernel,
        grid=grid,
        in_specs=[tile, _const_spec((1, d)), _const_spec((d, 3 * d)),
                  _const_spec((3, d)), _const_spec((d, d))] + [c[0] for c in cast_io],
        out_specs=[tile] + [c[1] for c in cast_io],
        out_shape=[jax.ShapeDtypeStruct(x.shape, F32)] + [c[2] for c in cast_io],
        scratch_shapes=[pltpu.VMEM((tm + SUBLANES, d), F32)],
        compiler_params=_params(2),
        name="short_conv_mixer",
    )(x, g, w_in, conv_w, w_out, *[w for w, _ in casts])
    return out[0], out[1:]


def _ffn_kernel(*refs, with_attn, with_final_norm):
    refs = list(refs)
    x_ref = refs.pop(0)
    if with_attn:
        a_ref, wo_ref = refs.pop(0), refs.pop(0)
    g_ref, wup_ref, cw_ref, cb_ref, wdown_ref = refs[:5]
    refs = refs[5:]
    if with_final_norm:
        gf_ref = refs.pop(0)
    n_cast = (len(refs) - 3) // 2
    o_ref, r_scr, acc_scr = refs[n_cast], refs[-2], refs[-1]
    _cast_blocks(refs[:n_cast], refs[n_cast + 1:-2])

    tm, d = x_ref.shape
    ff = wdown_ref.shape[0]
    fc = FF_CHUNK
    _load_carry(r_scr, tm)

    x = x_ref[...]
    if with_attn:
        x = x + _dot(a_ref[...], wo_ref[...])
    h = ((x * _rms_scale(x)) * g_ref[...]).astype(BF16)

    for c in range(ff // fc):
        for half in range(2):
            cols = slice(half * ff + c * fc, half * ff + (c + 1) * fc)
            r_scr[SUBLANES:SUBLANES + tm, cols] = _dot(h, wup_ref[:, cols])

    acc_scr[...] = x
    for c in range(ff // fc):
        gcols = slice(c * fc, (c + 1) * fc)
        vcols = slice(ff + c * fc, ff + (c + 1) * fc)
        ug = _causal_conv3(r_scr.at[:, gcols], cw_ref[:, gcols], 0, tm) + cb_ref[:, gcols]
        uv = _causal_conv3(r_scr.at[:, vcols], cw_ref[:, vcols], 0, tm) + cb_ref[:, vcols]
        act = (ug * jax.nn.sigmoid(ug) * uv).astype(BF16)
        acc_scr[...] += _dot(act, wdown_ref[gcols, :])

    y = acc_scr[...]
    if with_final_norm:
        y = (y * _rms_scale(y)) * gf_ref[...]
    o_ref[...] = y


def _ffn(x, g, w_up, conv_w, conv_b, w_down, attn=None, w_o=None, final_g=None, casts=()):
    bsz, s, d = x.shape
    ff = w_down.shape[0]
    tm = FFN_TOKEN_TILE
    grid = (bsz, s // tm)
    cast_io = [_cast_io(w, layer, grid) for w, layer in casts]
    tile = pl.BlockSpec((None, tm, d), lambda b, i: (b, i, 0))
    args, specs = [x], [tile]
    if attn is not None:
        args += [attn, w_o]
        specs += [tile, _const_spec((d, d))]
    args += [g, w_up, conv_w, conv_b, w_down]
    specs += [_const_spec((1, d)), _const_spec((d, 2 * ff)), _const_spec((3, 2 * ff)),
              _const_spec((1, 2 * ff)), _const_spec((ff, d))]
    if final_g is not None:
        args.append(final_g)
        specs.append(_const_spec((1, d)))
    kernel = functools.partial(_ffn_kernel, with_attn=attn is not None,
                               with_final_norm=final_g is not None)
    out = pl.pallas_call(
        kernel,
        grid=grid,
        in_specs=specs + [c[0] for c in cast_io],
        out_specs=[tile] + [c[1] for c in cast_io],
        out_shape=[jax.ShapeDtypeStruct(x.shape, F32)] + [c[2] for c in cast_io],
        scratch_shapes=[pltpu.VMEM((tm + SUBLANES, 2 * ff), F32),
                        pltpu.VMEM((tm, d), F32)],
        compiler_params=_params(2),
        name="conv_glu_ffn",
    )(*args, *[w for w, _ in casts])
    return out[0], out[1:]


def _qkv_kernel(x_ref, gq_ref, gkv_ref, wq_ref, wk_ref, wvt_ref, *rest):
    n_cast = (len(rest) - 4) // 2
    q_ref, k_ref, vt_ref, km_ref = rest[n_cast:n_cast + 4]
    _cast_blocks(rest[:n_cast], rest[n_cast + 4:])
    tm, d = x_ref.shape
    x = x_ref[...]
    xn = x * _rms_scale(x)
    hq = (xn * gq_ref[...]).astype(BF16)
    hkv = (xn * gkv_ref[...]).astype(BF16)
    q_ref[...] = (_dot(hq, wq_ref[...]) * (HEAD_DIM ** -0.5 * LOG2E)).astype(BF16)
    k = _dot(hkv, wk_ref[...])
    k_ref[...] = k.astype(BF16)
    nblk = tm // MOBA_BLOCK
    km_ref[...] = jnp.sum(k.reshape(nblk, MOBA_BLOCK, d), axis=1) / float(MOBA_BLOCK)
    vt_ref[...] = _dot_nt(wvt_ref[...], hkv).astype(BF16)


def _qkv(x, gq, gkv, w_q, w_k, w_vt, casts=()):
    bsz, s, d = x.shape
    tm = TOKEN_TILE
    nblk = tm // MOBA_BLOCK
    grid = (bsz, s // tm)
    cast_io = [_cast_io(w, layer, grid) for w, layer in casts]
    tile = pl.BlockSpec((None, tm, d), lambda b, i: (b, i, 0))
    out = pl.pallas_call(
        _qkv_kernel,
        grid=grid,
        in_specs=[tile, _const_spec((1, d)), _const_spec((1, d)),
                  _const_spec((d, d)), _const_spec((d, d)), _const_spec((d, d))]
        + [c[0] for c in cast_io],
        out_specs=[tile, tile,
                   pl.BlockSpec((None, d, tm), lambda b, i: (b, 0, i)),
                   pl.BlockSpec((None, None, nblk, d), lambda b, i: (b, i, 0, 0))]
        + [c[1] for c in cast_io],
        out_shape=[jax.ShapeDtypeStruct((bsz, s, d), BF16),
                   jax.ShapeDtypeStruct((bsz, s, d), BF16),
                   jax.ShapeDtypeStruct((bsz, d, s), BF16),
                   jax.ShapeDtypeStruct((bsz, s // tm, nblk, d), F32)]
        + [c[2] for c in cast_io],
        compiler_params=_params(2),
        name="qkv_proj",
    )(x, gq, gkv, w_q, w_k, w_vt, *[w for w, _ in casts])
    return out[0], out[1], out[2], out[3], out[4:]


def _bucket_starts():
    n = np.arange(0, 2 * MOBA_BLOCK)
    nf = np.maximum(n, 1).astype(np.float64)
    large = MAX_EXACT + (np.log(nf / MAX_EXACT) / math.log(REL_MAX_DIST / MAX_EXACT)
                         * (N_BUCKETS - MAX_EXACT)).astype(np.int64)
    bucket = np.where(n < MAX_EXACT, n, np.minimum(large, N_BUCKETS - 1))
    assert np.all(np.diff(bucket) >= 0) and bucket[REL_MAX_DIST] == N_BUCKETS - 1
    return [int(np.argmax(bucket >= b)) for b in range(N_BUCKETS)]


def _bias_kernel(rb_ref, o_ref):
    h = pl.program_id(0)
    starts = _bucket_starts()
    key = lax.broadcasted_iota(jnp.int32, (MOBA_BLOCK, MOBA_BLOCK), 0)
    qry = lax.broadcasted_iota(jnp.int32, (MOBA_BLOCK, MOBA_BLOCK), 1)
    for dblk in range(2):
        dist = qry - key + dblk * MOBA_BLOCK
        val = jnp.full(dist.shape, rb_ref[0, h], F32)
        for b in range(1, N_BUCKETS):
            val = jnp.where(dist >= starts[b], rb_ref[b, h], val)
        if dblk == 0:
            val = jnp.where(dist >= 0, val, NEG)
        o_ref[dblk] = val * LOG2E


def _bias_tiles(rel_bias):
    return pl.pallas_call(
        _bias_kernel,
        grid=(N_HEADS,),
        in_specs=[pl.BlockSpec(memory_space=pltpu.SMEM)],
        out_specs=pl.BlockSpec((None, 2, MOBA_BLOCK, MOBA_BLOCK), lambda h: (h, 0, 0, 0)),
        out_shape=jax.ShapeDtypeStruct((N_HEADS, 2, MOBA_BLOCK, MOBA_BLOCK), F32),
        compiler_params=_params(1),
        name="rel_bias_tiles",
    )(rel_bias)


def _moba_kernel(rb_ref, q_ref, k_ref, vt_ref, km_ref, bias_ref, o_ref,
                 kh_scr, va_scr, madd_scr, s_scr, p_scr):
    s = q_ref.shape[0]
    nb = s // MOBA_BLOCK
    blk = MOBA_BLOCK
    n_heads = va_scr.shape[0]
    first_head = pl.program_id(0) * n_heads
    units = [(i, h) for i in reversed(range(nb)) for h in range(n_heads)]
    lanes_of = lambda h: slice((h // HEADS_PER_LANE_BLOCK) * LANES,
                               (h // HEADS_PER_LANE_BLOCK + 1) * LANES)

    blk_id = lax.broadcasted_iota(jnp.int32, (nb, s), 0)
    own = jnp.right_shift(lax.broadcasted_iota(jnp.int32, (nb, s), 1), int(math.log2(blk)))
    past = blk_id < own
    ones_row = (lax.broadcasted_iota(jnp.int32, (V_AUG_ROWS - HEAD_DIM, s), 0) == 0).astype(BF16)
    for p in range(n_heads // HEADS_PER_LANE_BLOCK):
        cols = slice(p * LANES, (p + 1) * LANES)
        k = k_ref[:, cols]
        lane = lax.broadcasted_iota(jnp.int32, k.shape, 1)
        km = km_ref[:, cols]
        lane_b = lax.broadcasted_iota(jnp.int32, km.shape, 1)
        km_heads = []
        for half in range(HEADS_PER_LANE_BLOCK):
            h = p * HEADS_PER_LANE_BLOCK + half
            lo, hi = half * HEAD_DIM, (half + 1) * HEAD_DIM
            kh_scr[h] = jnp.where((lane >= lo) & (lane < hi), k, jnp.zeros_like(k))
            km_heads.append(jnp.where((lane_b >= lo) & (lane_b < hi), km, 0.0))
            va_scr[h, 0:HEAD_DIM, :] = vt_ref[h * HEAD_DIM:(h + 1) * HEAD_DIM, :]
            va_scr[h, HEAD_DIM:V_AUG_ROWS, :] = ones_row

        gate_t = _dot_nt(jnp.concatenate(km_heads, axis=0).astype(BF16), q_ref[:, cols])
        for half in range(HEADS_PER_LANE_BLOCK):
            g = jnp.where(past, gate_t[half * nb:(half + 1) * nb, :], NEG)
            beaten_by = jnp.zeros((nb, s), jnp.int32)
            for j in range(nb):
                row = g[j:j + 1, :]
                ahead = (row > g) | ((row == g) & (j < blk_id))
                beaten_by = beaten_by + ahead.astype(jnp.int32)
            chosen = past & (beaten_by < MOBA_TOP_K)
            madd_scr[p * HEADS_PER_LANE_BLOCK + half] = jnp.where(chosen, 0.0, NEG)

    far_bias = [rb_ref[N_BUCKETS - 1, first_head + h] * LOG2E for h in range(n_heads)]

    def block_offset(u, j):
        i, h = units[u]
        sel = madd_scr[h, j:j + 1, i * blk:(i + 1) * blk]
        return sel if j == i - 1 else sel + far_bias[h]

    def scores(u, j):
        i, h = units[u]
        rows = slice(j * blk, (j + 1) * blk)
        part = _dot_nt(kh_scr[h, rows, :], q_ref[i * blk:(i + 1) * blk, lanes_of(h)])
        if j >= i - 1:
            part = part + bias_ref[h, i - j]
        s_scr[u % 2, rows, :] = part
        pm = jnp.max(part.reshape(blk // SUBLANES, SUBLANES, blk), axis=0)
        return pm if j == i else pm + block_offset(u, j)

    def probs(u, j, m):
        i, _ = units[u]
        rows = slice(j * blk, (j + 1) * blk)
        shift = m if j == i else m - block_offset(u, j)
        p_scr[u % 2, rows, :] = jnp.exp2(s_scr[u % 2, rows, :] - shift).astype(BF16)

    def weighted_values(u):
        i, h = units[u]
        nk = (i + 1) * blk
        o_aug = _dot(va_scr[h, :, 0:nk], p_scr[u % 2, 0:nk, :])
        return o_aug[0:HEAD_DIM, :] / o_aug[HEAD_DIM:HEAD_DIM + 1, :]

    col_max = [None] * len(units)
    outs = []
    for step in range(len(units) + 2):
        if step >= 2:
            outs.append(weighted_values(step - 2))
            if len(outs) == HEADS_PER_LANE_BLOCK:
                i, h = units[step - 2]
                o_ref[i * blk:(i + 1) * blk, lanes_of(h)] = (
                    jnp.concatenate(outs, axis=0).T.astype(o_ref.dtype))
                outs = []
        n_new = units[step][0] + 1 if step < len(units) else 0
        n_old = units[step - 1][0] + 1 if 1 <= step <= len(units) else 0
        m8 = None
        for j in range(max(n_new, n_old)):
            if j < n_new:
                pm = scores(step, j)
                m8 = pm if m8 is None else jnp.maximum(m8, pm)
            if j < n_old:
                probs(step - 1, j, col_max[step - 1])
        if n_new:
            col_max[step] = jnp.max(m8, axis=0, keepdims=True)


def _moba(rel_bias, q, k, vt, kmean, bias_tiles):
    bsz, s, d = q.shape
    nb = s // MOBA_BLOCK
    hs = MOBA_HEADS_PER_STEP
    width = hs * HEAD_DIM
    col = pl.BlockSpec((None, s, width), lambda p, b: (b, 0, p))
    return pl.pallas_call(
        _moba_kernel,
        grid=(N_HEADS // hs, bsz),
        in_specs=[pl.BlockSpec(memory_space=pltpu.SMEM),
                  col, col,
                  pl.BlockSpec((None, width, s), lambda p, b: (b, p, 0)),
                  pl.BlockSpec((None, nb, width), lambda p, b: (b, 0, p)),
                  pl.BlockSpec((hs, 2, MOBA_BLOCK, MOBA_BLOCK), lambda p, b: (p, 0, 0, 0))],
        out_specs=col,
        out_shape=jax.ShapeDtypeStruct((bsz, s, d), BF16),
        scratch_shapes=[pltpu.VMEM((hs, s, LANES), BF16),
                        pltpu.VMEM((hs, V_AUG_ROWS, s), BF16),
                        pltpu.VMEM((hs, nb, s), F32),
                        pltpu.VMEM((2, s, MOBA_BLOCK), F32),
                        pltpu.VMEM((2, s, MOBA_BLOCK), BF16)],
        compiler_params=_params(2),
        name="moba_attention",
    )(rel_bias, q, k, vt, kmean, bias_tiles)


def kernel(x, a_norm, a_w_in, a_conv, a_w_out, kv_norm, w_k, w_v, b_norm, b_w_q, b_w_o,
           rel_bias, f_norm, f_w_up, f_conv, f_conv_b, f_w_down, final_norm):
    bsz, s, d = x.shape
    assert (s % TOKEN_TILE == 0 and s % FFN_TOKEN_TILE == 0 and TOKEN_TILE % MOBA_BLOCK == 0
            and d == D_MODEL and N_HEADS % MOBA_HEADS_PER_STEP == 0
            and MOBA_HEADS_PER_STEP % HEADS_PER_LANE_BLOCK == 0
            and a_norm.shape[0] == 1 and b_norm.shape[0] == 1 and D_FF % FF_CHUNK == 0)
    row = lambda v: v.reshape(1, -1)
    bf = lambda w: w.astype(BF16)

    x, (w_up0, w_down0) = _mixer(x, row(a_norm[0]), bf(a_w_in[0]), a_conv[0], bf(a_w_out[0]),
                                 casts=[(f_w_up, 0), (f_w_down, 0)])
    x, (w_q, w_kk, w_o) = _ffn(x, row(f_norm[0]), w_up0, f_conv[0], row(f_conv_b[0]), w_down0,
                               casts=[(b_w_q, 0), (w_k, None), (b_w_o, 0)])

    q, k, vt, kmean, (w_up1, w_down1) = _qkv(x, row(b_norm[0]), row(kv_norm), w_q, w_kk, bf(w_v.T),
                                             casts=[(f_w_up, 1), (f_w_down, 1)])
    kmean = kmean.reshape(bsz, s // MOBA_BLOCK, d)
    attn = _moba(rel_bias, q, k, vt, kmean, _bias_tiles(rel_bias))

    out, _ = _ffn(x, row(f_norm[1]), w_up1, f_conv[1], row(f_conv_b[1]), w_down1,
                  attn=attn, w_o=w_o, final_g=row(final_norm))
    return out
```

```python
import functools
import math

import numpy as np
import jax
import jax.numpy as jnp
from jax import lax
from jax.experimental import pallas as pl
from jax.experimental.pallas import tpu as pltpu

D_MODEL = 1024
N_HEADS = 16
HEAD_DIM = 64
MOBA_BLOCK = 256
MOBA_TOP_K = 3
N_BUCKETS = 32
MAX_EXACT = N_BUCKETS // 2
REL_MAX_DIST = 128
D_FF = 2816
EPS = 1e-6
NEG = -1e30
LOG2E = math.log2(math.e)

SUBLANES = 8
LANES = 128
HEADS_PER_LANE_BLOCK = LANES // HEAD_DIM
MOBA_HEADS_PER_STEP = 4
V_AUG_ROWS = HEAD_DIM + 16
VMEM_LIMIT_BYTES = 56 * 1024 * 1024

TOKEN_TILE = 1024
FFN_TOKEN_TILE = 512
FF_CHUNK = 256

BF16 = jnp.bfloat16
F32 = jnp.float32


def _dot(a, b):
    return jnp.dot(a, b, preferred_element_type=F32)


def _dot_nt(a, b):
    return lax.dot_general(a, b, (((1,), (1,)), ((), ())), preferred_element_type=F32)


def _rms_scale(x):
    return lax.rsqrt(jnp.mean(x * x, axis=-1, keepdims=True) + EPS)


def _const_spec(shape):
    return pl.BlockSpec(shape, lambda *_: (0,) * len(shape), pipeline_mode=pl.Buffered(1))


def _params(n_axes):
    return pltpu.CompilerParams(
        dimension_semantics=("arbitrary",) * n_axes,
        vmem_limit_bytes=VMEM_LIMIT_BYTES)


def _load_carry(scr, tm):
    i = pl.program_id(1)

    @pl.when(i == 0)
    def _():
        scr[0:SUBLANES, :] = jnp.zeros((SUBLANES, scr.shape[1]), F32)

    @pl.when(i > 0)
    def _():
        scr[0:SUBLANES, :] = scr[tm:tm + SUBLANES, :]


def _causal_conv3(scr, w, r0, nrows):
    base = SUBLANES + r0
    return (w[0:1, :] * scr[base - 2:base - 2 + nrows, :]
            + w[1:2, :] * scr[base - 1:base - 1 + nrows, :]
            + w[2:3, :] * scr[base:base + nrows, :])


def _cast_io(w, layer, grid):
    rows, cols = w.shape[-2:]
    n_steps = grid[0] * grid[1]
    bf16_rows = 2 * SUBLANES
    n_blocks = max(n for n in range(1, n_steps + 1)
                   if rows % n == 0 and (rows // n) % bf16_rows == 0)
    block_rows = rows // n_blocks

    def index(b, i):
        return jnp.minimum(b * grid[1] + i, n_blocks - 1), 0

    if w.ndim == 3:
        in_spec = pl.BlockSpec((None, block_rows, cols), lambda b, i: (layer,) + index(b, i))
    else:
        in_spec = pl.BlockSpec((block_rows, cols), index)
    return in_spec, pl.BlockSpec((block_rows, cols), index), jax.ShapeDtypeStruct((rows, cols), BF16)


def _cast_blocks(src_refs, dst_refs):
    for src, dst in zip(src_refs, dst_refs):
        dst[...] = src[...].astype(BF16)


def _mixer_kernel(x_ref, g_ref, win_ref, cw_ref, wout_ref, *rest):
    n_cast = (len(rest) - 2) // 2
    o_ref, z_scr = rest[n_cast], rest[-1]
    _cast_blocks(rest[:n_cast], rest[n_cast + 1:-1])
    tm, d = x_ref.shape
    _load_carry(z_scr, tm)
    x = x_ref[...]
    h = ((x * _rms_scale(x)) * g_ref[...]).astype(BF16)
    b_gate = _dot(h, win_ref[:, 0:d])
    c_gate = _dot(h, win_ref[:, d:2 * d])
    hx = _dot(h, win_ref[:, 2 * d:3 * d])
    z_scr[SUBLANES:SUBLANES + tm, :] = c_gate * hx
    y = (b_gate * _causal_conv3(z_scr, cw_ref[...], 0, tm)).astype(BF16)
    o_ref[...] = x + _dot(y, wout_ref[...])


def _mixer(x, g, w_in, conv_w, w_out, casts=()):
    bsz, s, d = x.shape
    tm = TOKEN_TILE
    grid = (bsz, s // tm)
    tile = pl.BlockSpec((None, tm, d), lambda b, i: (b, i, 0))
    cast_io = [_cast_io(w, layer, grid) for w, layer in casts]
    out = pl.pallas_call(
        _mixer_kernel,
        grid=grid,
        in_specs=[tile, _const_spec((1, d)), _const_spec((d, 3 * d)),
                  _const_spec((3, d)), _const_spec((d, d))] + [c[0] for c in cast_io],
        out_specs=[tile] + [c[1] for c in cast_io],
        out_shape=[jax.ShapeDtypeStruct(x.shape, F32)] + [c[2] for c in cast_io],
        scratch_shapes=[pltpu.VMEM((tm + SUBLANES, d), F32)],
        compiler_params=_params(2),
        name="short_conv_mixer",
    )(x, g, w_in, conv_w, w_out, *[w for w, _ in casts])
    return out[0], out[1:]


def _ffn_kernel(*refs, with_attn, with_final_norm):
    refs = list(refs)
    x_ref = refs.pop(0)
    if with_attn:
        a_ref, wo_ref = refs.pop(0), refs.pop(0)
    g_ref, wup_ref, cw_ref, cb_ref, wdown_ref = refs[:5]
    refs = refs[5:]
    if with_final_norm:
        gf_ref = refs.pop(0)
    n_cast = (len(refs) - 3) // 2
    o_ref, r_scr, acc_scr = refs[n_cast], refs[-2], refs[-1]
    _cast_blocks(refs[:n_cast], refs[n_cast + 1:-2])

    tm, d = x_ref.shape
    ff = wdown_ref.shape[0]
    fc = FF_CHUNK
    _load_carry(r_scr, tm)

    x = x_ref[...]
    if with_attn:
        x = x + _dot(a_ref[...], wo_ref[...])
    h = ((x * _rms_scale(x)) * g_ref[...]).astype(BF16)

    for c in range(ff // fc):
        for half in range(2):
            cols = slice(half * ff + c * fc, half * ff + (c + 1) * fc)
            r_scr[SUBLANES:SUBLANES + tm, cols] = _dot(h, wup_ref[:, cols])

    acc_scr[...] = x
    for c in range(ff // fc):
        gcols = slice(c * fc, (c + 1) * fc)
        vcols = slice(ff + c * fc, ff + (c + 1) * fc)
        ug = _causal_conv3(r_scr.at[:, gcols], cw_ref[:, gcols], 0, tm) + cb_ref[:, gcols]
        uv = _causal_conv3(r_scr.at[:, vcols], cw_ref[:, vcols], 0, tm) + cb_ref[:, vcols]
        act = (ug * jax.nn.sigmoid(ug) * uv).astype(BF16)
        acc_scr[...] += _dot(act, wdown_ref[gcols, :])

    y = acc_scr[...]
    if with_final_norm:
        y = (y * _rms_scale(y)) * gf_ref[...]
    o_ref[...] = y


def _ffn(x, g, w_up, conv_w, conv_b, w_down, attn=None, w_o=None, final_g=None, casts=()):
    bsz, s, d = x.shape
    ff = w_down.shape[0]
    tm = FFN_TOKEN_TILE
    grid = (bsz, s // tm)
    cast_io = [_cast_io(w, layer, grid) for w, layer in casts]
    tile = pl.BlockSpec((None, tm, d), lambda b, i: (b, i, 0))
    args, specs = [x], [tile]
    if attn is not None:
        args += [attn, w_o]
        specs += [tile, _const_spec((d, d))]
    args += [g, w_up, conv_w, conv_b, w_down]
    specs += [_const_spec((1, d)), _const_spec((d, 2 * ff)), _const_spec((3, 2 * ff)),
              _const_spec((1, 2 * ff)), _const_spec((ff, d))]
    if final_g is not None:
        args.append(final_g)
        specs.append(_const_spec((1, d)))
    kernel = functools.partial(_ffn_kernel, with_attn=attn is not None,
                               with_final_norm=final_g is not None)
    out = pl.pallas_call(
        kernel,
        grid=grid,
        in_specs=specs + [c[0] for c in cast_io],
        out_specs=[tile] + [c[1] for c in cast_io],
        out_shape=[jax.ShapeDtypeStruct(x.shape, F32)] + [c[2] for c in cast_io],
        scratch_shapes=[pltpu.VMEM((tm + SUBLANES, 2 * ff), F32),
                        pltpu.VMEM((tm, d), F32)],
        compiler_params=_params(2),
        name="conv_glu_ffn",
    )(*args, *[w for w, _ in casts])
    return out[0], out[1:]


def _qkv_kernel(x_ref, gq_ref, gkv_ref, wq_ref, wk_ref, wvt_ref, *rest):
    n_cast = (len(rest) - 4) // 2
    q_ref, k_ref, vt_ref, km_ref = rest[n_cast:n_cast + 4]
    _cast_blocks(rest[:n_cast], rest[n_cast + 4:])
    tm, d = x_ref.shape
    x = x_ref[...]
    xn = x * _rms_scale(x)
    hq = (xn * gq_ref[...]).astype(BF16)
    hkv = (xn * gkv_ref[...]).astype(BF16)
    q_ref[...] = (_dot(hq, wq_ref[...]) * (HEAD_DIM ** -0.5 * LOG2E)).astype(BF16)
    k = _dot(hkv, wk_ref[...])
    k_ref[...] = k.astype(BF16)
    nblk = tm // MOBA_BLOCK
    km_ref[...] = jnp.sum(k.reshape(nblk, MOBA_BLOCK, d), axis=1) / float(MOBA_BLOCK)
    vt_ref[...] = _dot_nt(wvt_ref[...], hkv).astype(BF16)


def _qkv(x, gq, gkv, w_q, w_k, w_vt, casts=()):
    bsz, s, d = x.shape
    tm = TOKEN_TILE
    nblk = tm // MOBA_BLOCK
    grid = (bsz, s // tm)
    cast_io = [_cast_io(w, layer, grid) for w, layer in casts]
    tile = pl.BlockSpec((None, tm, d), lambda b, i: (b, i, 0))
    out = pl.pallas_call(
        _qkv_kernel,
        grid=grid,
        in_specs=[tile, _const_spec((1, d)), _const_spec((1, d)),
                  _const_spec((d, d)), _const_spec((d, d)), _const_spec((d, d))]
        + [c[0] for c in cast_io],
        out_specs=[tile, tile,
                   pl.BlockSpec((None, d, tm), lambda b, i: (b, 0, i)),
                   pl.BlockSpec((None, None, nblk, d), lambda b, i: (b, i, 0, 0))]
        + [c[1] for c in cast_io],
        out_shape=[jax.ShapeDtypeStruct((bsz, s, d), BF16),
                   jax.ShapeDtypeStruct((bsz, s, d), BF16),
                   jax.ShapeDtypeStruct((bsz, d, s), BF16),
                   jax.ShapeDtypeStruct((bsz, s // tm, nblk, d), F32)]
        + [c[2] for c in cast_io],
        compiler_params=_params(2),
        name="qkv_proj",
    )(x, gq, gkv, w_q, w_k, w_vt, *[w for w, _ in casts])
    return out[0], out[1], out[2], out[3], out[4:]


def _bucket_starts():
    n = np.arange(0, 2 * MOBA_BLOCK)
    nf = np.maximum(n, 1).astype(np.float64)
    large = MAX_EXACT + (np.log(nf / MAX_EXACT) / math.log(REL_MAX_DIST / MAX_EXACT)
                         * (N_BUCKETS - MAX_EXACT)).astype(np.int64)
    bucket = np.where(n < MAX_EXACT, n, np.minimum(large, N_BUCKETS - 1))
    assert np.all(np.diff(bucket) >= 0) and bucket[REL_MAX_DIST] == N_BUCKETS - 1
    return [int(np.argmax(bucket >= b)) for b in range(N_BUCKETS)]


def _bias_kernel(rb_ref, o_ref):
    h = pl.program_id(0)
    starts = _bucket_starts()
    key = lax.broadcasted_iota(jnp.int32, (MOBA_BLOCK, MOBA_BLOCK), 0)
    qry = lax.broadcasted_iota(jnp.int32, (MOBA_BLOCK, MOBA_BLOCK), 1)
    for dblk in range(2):
        dist = qry - key + dblk * MOBA_BLOCK
        val = jnp.full(dist.shape, rb_ref[0, h], F32)
        for b in range(1, N_BUCKETS):
            val = jnp.where(dist >= starts[b], rb_ref[b, h], val)
        if dblk == 0:
            val = jnp.where(dist >= 0, val, NEG)
        o_ref[dblk] = val * LOG2E


def _bias_tiles(rel_bias):
    return pl.pallas_call(
        _bias_kernel,
        grid=(N_HEADS,),
        in_specs=[pl.BlockSpec(memory_space=pltpu.SMEM)],
        out_specs=pl.BlockSpec((None, 2, MOBA_BLOCK, MOBA_BLOCK), lambda h: (h, 0, 0, 0)),
        out_shape=jax.ShapeDtypeStruct((N_HEADS, 2, MOBA_BLOCK, MOBA_BLOCK), F32),
        compiler_params=_params(1),
        name="rel_bias_tiles",
    )(rel_bias)


def _moba_kernel(rb_ref, q_ref, k_ref, vt_ref, km_ref, bias_ref, o_ref,
                 kh_scr, va_scr, madd_scr, s_scr, p_scr):
    s = q_ref.shape[0]
    nb = s // MOBA_BLOCK
    blk = MOBA_BLOCK
    n_heads = va_scr.shape[0]
    first_head = pl.program_id(0) * n_heads
    units = [(i, h) for i in reversed(range(nb)) for h in range(n_heads)]
    lanes_of = lambda h: slice((h // HEADS_PER_LANE_BLOCK) * LANES,
                               (h // HEADS_PER_LANE_BLOCK + 1) * LANES)

    blk_id = lax.broadcasted_iota(jnp.int32, (nb, s), 0)
    own = jnp.right_shift(lax.broadcasted_iota(jnp.int32, (nb, s), 1), int(math.log2(blk)))
    past = blk_id < own
    ones_row = (lax.broadcasted_iota(jnp.int32, (V_AUG_ROWS - HEAD_DIM, s), 0) == 0).astype(BF16)
    for p in range(n_heads // HEADS_PER_LANE_BLOCK):
        cols = slice(p * LANES, (p + 1) * LANES)
        k = k_ref[:, cols]
        lane = lax.broadcasted_iota(jnp.int32, k.shape, 1)
        km = km_ref[:, cols]
        lane_b = lax.broadcasted_iota(jnp.int32, km.shape, 1)
        km_heads = []
        for half in range(HEADS_PER_LANE_BLOCK):
            h = p * HEADS_PER_LANE_BLOCK + half
            lo, hi = half * HEAD_DIM, (half + 1) * HEAD_DIM
            kh_scr[h] = jnp.where((lane >= lo) & (lane < hi), k, jnp.zeros_like(k))
            km_heads.append(jnp.where((lane_b >= lo) & (lane_b < hi), km, 0.0))
            va_scr[h, 0:HEAD_DIM, :] = vt_ref[h * HEAD_DIM:(h + 1) * HEAD_DIM, :]
            va_scr[h, HEAD_DIM:V_AUG_ROWS, :] = ones_row

        gate_t = _dot_nt(jnp.concatenate(km_heads, axis=0).astype(BF16), q_ref[:, cols])
        for half in range(HEADS_PER_LANE_BLOCK):
            g = jnp.where(past, gate_t[half * nb:(half + 1) * nb, :], NEG)
            beaten_by = jnp.zeros((nb, s), jnp.int32)
            for j in range(nb):
                row = g[j:j + 1, :]
                ahead = (row > g) | ((row == g) & (j < blk_id))
                beaten_by = beaten_by + ahead.astype(jnp.int32)
            chosen = past & (beaten_by < MOBA_TOP_K)
            madd_scr[p * HEADS_PER_LANE_BLOCK + half] = jnp.where(chosen, 0.0, NEG)

    far_bias = [rb_ref[N_BUCKETS - 1, first_head + h] * LOG2E for h in range(n_heads)]

    def block_offset(u, j):
        i, h = units[u]
        sel = madd_scr[h, j:j + 1, i * blk:(i + 1) * blk]
        return sel if j == i - 1 else sel + far_bias[h]

    def scores(u, j):
        i, h = units[u]
        rows = slice(j * blk, (j + 1) * blk)
        s_scr[u % 3, rows, :] = _dot_nt(kh_scr[h, rows, :], q_ref[i * blk:(i + 1) * blk, lanes_of(h)])

    def biased(u, j):
        i, h = units[u]
        part = s_scr[u % 3, j * blk:(j + 1) * blk, :]
        return part + bias_ref[h, i - j] if j >= i - 1 else part

    def col_maxima(u, j):
        i, _ = units[u]
        pm = jnp.max(biased(u, j).reshape(blk // SUBLANES, SUBLANES, blk), axis=0)
        return pm if j == i else pm + block_offset(u, j)

    def probs(u, j, m):
        i, _ = units[u]
        shift = m if j == i else m - block_offset(u, j)
        p_scr[u % 2, j * blk:(j + 1) * blk, :] = jnp.exp2(biased(u, j) - shift).astype(BF16)

    def weighted_values(u, j):
        _, h = units[u]
        rows = slice(j * blk, (j + 1) * blk)
        return _dot(va_scr[h, :, rows], p_scr[u % 2, rows, :])

    n_units = len(units)
    blocks_of = lambda u: units[u][0] + 1 if 0 <= u < n_units else 0
    col_max = [None] * n_units
    outs = []
    for step in range(n_units + 3):
        m8 = None
        o_aug = None
        for j in range(max(blocks_of(step - lag) for lag in range(4))):
            if j < blocks_of(step - 2):
                probs(step - 2, j, col_max[step - 2])
            if j < blocks_of(step):
                scores(step, j)
            if j < blocks_of(step - 1):
                pm = col_maxima(step - 1, j)
                m8 = pm if m8 is None else jnp.maximum(m8, pm)
            if j < blocks_of(step - 3):
                part = weighted_values(step - 3, j)
                o_aug = part if o_aug is None else o_aug + part
        if blocks_of(step - 1):
            col_max[step - 1] = jnp.max(m8, axis=0, keepdims=True)
        if blocks_of(step - 3):
            outs.append(o_aug[0:HEAD_DIM, :] / o_aug[HEAD_DIM:HEAD_DIM + 1, :])
            if len(outs) == HEADS_PER_LANE_BLOCK:
                i, h = units[step - 3]
                o_ref[i * blk:(i + 1) * blk, lanes_of(h)] = (
                    jnp.concatenate(outs, axis=0).T.astype(o_ref.dtype))
                outs = []


def _moba(rel_bias, q, k, vt, kmean, bias_tiles):
    bsz, s, d = q.shape
    nb = s // MOBA_BLOCK
    hs = MOBA_HEADS_PER_STEP
    width = hs * HEAD_DIM
    col = pl.BlockSpec((None, s, width), lambda p, b: (b, 0, p))
    return pl.pallas_call(
        _moba_kernel,
        grid=(N_HEADS // hs, bsz),
        in_specs=[pl.BlockSpec(memory_space=pltpu.SMEM),
                  col, col,
                  pl.BlockSpec((None, width, s), lambda p, b: (b, p, 0)),
                  pl.BlockSpec((None, nb, width), lambda p, b: (b, 0, p)),
                  pl.BlockSpec((hs, 2, MOBA_BLOCK, MOBA_BLOCK), lambda p, b: (p, 0, 0, 0))],
        out_specs=col,
        out_shape=jax.ShapeDtypeStruct((bsz, s, d), BF16),
        scratch_shapes=[pltpu.VMEM((hs, s, LANES), BF16),
                        pltpu.VMEM((hs, V_AUG_ROWS, s), BF16),
                        pltpu.VMEM((hs, nb, s), F32),
                        pltpu.VMEM((3, s, MOBA_BLOCK), F32),
                        pltpu.VMEM((2, s, MOBA_BLOCK), BF16)],
        compiler_params=_params(2),
        name="moba_attention",
    )(rel_bias, q, k, vt, kmean, bias_tiles)


def kernel(x, a_norm, a_w_in, a_conv, a_w_out, kv_norm, w_k, w_v, b_norm, b_w_q, b_w_o,
           rel_bias, f_norm, f_w_up, f_conv, f_conv_b, f_w_down, final_norm):
    bsz, s, d = x.shape
    assert (s % TOKEN_TILE == 0 and s % FFN_TOKEN_TILE == 0 and TOKEN_TILE % MOBA_BLOCK == 0
            and d == D_MODEL and N_HEADS % MOBA_HEADS_PER_STEP == 0
            and MOBA_HEADS_PER_STEP % HEADS_PER_LANE_BLOCK == 0
            and a_norm.shape[0] == 1 and b_norm.shape[0] == 1 and D_FF % FF_CHUNK == 0)
    row = lambda v: v.reshape(1, -1)
    bf = lambda w: w.astype(BF16)

    x, (w_up0, w_down0) = _mixer(x, row(a_norm[0]), bf(a_w_in[0]), a_conv[0], bf(a_w_out[0]),
                                 casts=[(f_w_up, 0), (f_w_down, 0)])
    x, (w_q, w_kk, w_o) = _ffn(x, row(f_norm[0]), w_up0, f_conv[0], row(f_conv_b[0]), w_down0,
                               casts=[(b_w_q, 0), (w_k, None), (b_w_o, 0)])

    q, k, vt, kmean, (w_up1, w_down1) = _qkv(x, row(b_norm[0]), row(kv_norm), w_q, w_kk, bf(w_v.T),
                                             casts=[(f_w_up, 1), (f_w_down, 1)])
    kmean = kmean.reshape(bsz, s // MOBA_BLOCK, d)
    attn = _moba(rel_bias, q, k, vt, kmean, _bias_tiles(rel_bias))

    out, _ = _ffn(x, row(f_norm[1]), w_up1, f_conv[1], row(f_conv_b[1]), w_down1,
                  attn=attn, w_o=w_o, final_g=row(final_norm))
    return out
```
